```python
import math
import jax
import jax.numpy as jnp
from jax import lax
import numpy as np

D_MODEL = 1024
BATCH = 2
SEQ = 8192
DEPTH = 2

NORM_EPS = 1e-6
PLE_DIM = 256
D_FF = 2816
FFN_HALF = 0.5
BRANCH_WIDTH = D_MODEL // 2
N_BRANCHES = 3
MASK_VALUE = -1e30

ATTN_HEAD_DIM = 64
ATTN_HEADS = BRANCH_WIDTH // ATTN_HEAD_DIM
ATTN_KV_HEADS = ATTN_HEADS // 4
ATTN_GROUP = ATTN_HEADS // ATTN_KV_HEADS
WINDOW = 128
ATTN_BLOCK = 128
ATTN_SCALE = ATTN_HEAD_DIM ** -0.5
ATTN_IN = (ATTN_HEADS + 2 * ATTN_KV_HEADS) * ATTN_HEAD_DIM

RWKV_HEAD_DIM = 64
RWKV_HEADS = BRANCH_WIDTH // RWKV_HEAD_DIM
RWKV_W_RANK = 64
RWKV_A_RANK = 64
RWKV_G_RANK = 128
RWKV_GN_EPS = 64e-5
RWKV_DECAY_SCALE = math.exp(-0.5)
RWKV_WIDTHS = (BRANCH_WIDTH, BRANCH_WIDTH, BRANCH_WIDTH, RWKV_W_RANK, RWKV_A_RANK, RWKV_G_RANK)
RWKV_OFFSETS = tuple(int(o) for o in np.cumsum(RWKV_WIDTHS)[:-1])
RWKV_IN = sum(RWKV_WIDTHS)

HGRN_KEY_DIM = 128
HGRN_HEADS = BRANCH_WIDTH // HGRN_KEY_DIM
HGRN_VAL_DIM = BRANCH_WIDTH // HGRN_HEADS
HGRN_CHUNK = 64
HGRN_IN = 4 * BRANCH_WIDTH

GATE_IN = N_BRANCHES * D_MODEL
GROUP_WIDTHS = (ATTN_IN, RWKV_IN, HGRN_IN, GATE_IN)
GROUP_OFFSETS = tuple(int(o) for o in np.cumsum(GROUP_WIDTHS)[:-1])
IN_WIDTH = sum(GROUP_WIDTHS)

kernel_name = 'hybrid_gated_swa_rwkv7_hgrn2_block'


def rmsnorm(x, g, eps=NORM_EPS):
    xf = x.astype(jnp.float32)
    y = xf * lax.rsqrt(jnp.mean(xf * xf, axis=-1, keepdims=True) + eps)
    return (y * g.astype(jnp.float32)).astype(x.dtype)


def swiglu(x, w_gate, w_up, w_down):
    return (jax.nn.silu(x @ w_gate) * (x @ w_up)) @ w_down


def sliding_window_attention(q, k, v, sinks):
    B, L, _ = q.shape
    nb = L // ATTN_BLOCK
    qb = q.reshape(B, nb, ATTN_BLOCK, ATTN_KV_HEADS, ATTN_GROUP, ATTN_HEAD_DIM)

    def band(t):
        t = t.reshape(B, nb, ATTN_BLOCK, ATTN_KV_HEADS, ATTN_HEAD_DIM)
        prev = jnp.pad(t, ((0, 0), (1, 0), (0, 0), (0, 0), (0, 0)))[:, :-1]
        return jnp.concatenate([prev, t], axis=2)

    kb, vb = band(k), band(v)
    scores = jnp.einsum('bnqhgd,bnkhd->bnhgqk', qb, kb).astype(jnp.float32) * ATTN_SCALE
    qi = jnp.arange(ATTN_BLOCK)[:, None]
    kj = jnp.arange(2 * ATTN_BLOCK)[None, :]
    rel = ATTN_BLOCK + qi - kj
    in_band = (rel >= 0) & (rel < WINDOW)
    not_pad = (jnp.arange(nb)[:, None, None] > 0) | (kj[None] >= ATTN_BLOCK)
    mask = in_band[None] & not_pad
    scores = jnp.where(mask[None, :, None, None], scores, MASK_VALUE)
    sink = sinks.astype(jnp.float32).reshape(ATTN_KV_HEADS, ATTN_GROUP)[None, None, :, :, None, None]
    m = jnp.maximum(jnp.max(scores, axis=-1, keepdims=True), sink)
    pr = jnp.exp(scores - m)
    denom = jnp.sum(pr, axis=-1, keepdims=True) + jnp.exp(sink - m)
    out = jnp.einsum('bnhgqk,bnkhd->bnqhgd', (pr / denom).astype(v.dtype), vb)
    return out.reshape(B, L, ATTN_HEADS * ATTN_HEAD_DIM)


def rwkv7_mix(z, mu, w0, w_up, a0, a_up, g_up, k_k, k_a, r_k, ln_w, ln_b):
    B, L, _ = z.shape
    f32 = jnp.float32
    z = z.astype(f32)
    z_prev = jnp.pad(z, ((0, 0), (1, 0), (0, 0)))[:, :-1]
    z = z + mu * (z_prev - z)
    r, k, v, xw, xa, xg = jnp.split(z, RWKV_OFFSETS, axis=-1)
    d = w0 + jnp.tanh(xw) @ w_up
    decay = jnp.exp(-RWKV_DECAY_SCALE * jax.nn.sigmoid(d))
    a = jax.nn.sigmoid(a0 + xa @ a_up)
    g = jax.nn.sigmoid(xg) @ g_up
    heads = lambda t: t.reshape(B, L, RWKV_HEADS, RWKV_HEAD_DIM)
    kk = heads(k * k_k)
    kk = kk * lax.rsqrt(jnp.maximum(jnp.sum(kk * kk, axis=-1, keepdims=True), 1e-24))
    k = k * (1.0 + (a - 1.0) * k_a)
    r_h, k_h, v_h, a_h, w_h = heads(r), heads(k), heads(v), heads(a), heads(decay)
    b_h = kk * a_h

    def step(S, inp):
        r_t, w_t, k_t, v_t, kk_t, b_t = inp
        sa = jnp.einsum('bhvk,bhk->bhv', S, -kk_t)
        S = S * w_t[:, :, None, :] + sa[..., None] * b_t[:, :, None, :] + v_t[..., None] * k_t[:, :, None, :]
        return S, jnp.einsum('bhvk,bhk->bhv', S, r_t)

    xs = tuple(jnp.moveaxis(t, 1, 0) for t in (r_h, w_h, k_h, v_h, kk, b_h))
    S0 = jnp.zeros((B, RWKV_HEADS, RWKV_HEAD_DIM, RWKV_HEAD_DIM), f32)
    _, y = lax.scan(step, S0, xs)
    y = jnp.moveaxis(y, 0, 1)
    mean = jnp.mean(y, axis=-1, keepdims=True)
    var = jnp.mean(jnp.square(y - mean), axis=-1, keepdims=True)
    y = ((y - mean) * lax.rsqrt(var + RWKV_GN_EPS)).reshape(B, L, BRANCH_WIDTH) * ln_w + ln_b
    bonus = jnp.sum(r_h * k_h * r_k, axis=-1, keepdims=True) * v_h
    y = y + bonus.reshape(B, L, BRANCH_WIDTH)
    return y * g


def hgrn2_mix(z, lb, norm_g):
    B, L, _ = z.shape
    f32 = jnp.float32
    z = z.astype(f32)
    q, f, i, og = jnp.split(z, 4, axis=-1)
    lb = lb.astype(f32)
    forget = lb + (1.0 - lb) * jax.nn.sigmoid(f)
    log_f = jnp.log(forget)
    key = 1.0 - forget
    q = jax.nn.silu(q)
    nc = L // HGRN_CHUNK

    def chunks(t, dim):
        return t.reshape(B, nc, HGRN_CHUNK, HGRN_HEADS, dim).transpose(1, 0, 3, 2, 4)

    xs = (chunks(q, HGRN_KEY_DIM), chunks(key, HGRN_KEY_DIM), chunks(i, HGRN_VAL_DIM), chunks(log_f, HGRN_KEY_DIM))
    causal = jnp.tril(jnp.ones((HGRN_CHUNK, HGRN_CHUNK), bool))[:, :, None]

    def step(S, inp):
        q_c, k_c, v_c, g_c = inp
        b = jnp.cumsum(g_c, axis=-2)
        rel = b[..., :, None, :] - b[..., None, :, :]
        dec = jnp.where(causal, jnp.exp(jnp.where(causal, rel, 0.0)), 0.0)
        attn = jnp.einsum('bhtk,bhsk,bhtsk->bhts', q_c, k_c, dec)
        o = attn @ v_c + jnp.einsum('bhtk,bhkv->bhtv', q_c * jnp.exp(b), S)
        b_last = b[..., -1:, :]
        S = jnp.exp(b_last[..., 0, :])[..., None] * S + jnp.einsum('bhsk,bhsv->bhkv', k_c * jnp.exp(b_last - b), v_c)
        return S, o

    S0 = jnp.zeros((B, HGRN_HEADS, HGRN_KEY_DIM, HGRN_VAL_DIM), f32)
    _, o = lax.scan(step, S0, xs)
    o = o.transpose(1, 0, 3, 2, 4).reshape(B, L, HGRN_HEADS, HGRN_VAL_DIM)
    o = rmsnorm(o, norm_g).reshape(B, L, BRANCH_WIDTH)
    return o * jax.nn.silu(og)


def token_mixer(u, w_in, attn_sinks, rwkv_mu, rwkv_w0, rwkv_w_up, rwkv_a0, rwkv_a_up, rwkv_g_up,
                rwkv_k_k, rwkv_k_a, rwkv_r_k, rwkv_ln_w, rwkv_ln_b, lb, hgrn_norm, w_branch, w_o):
    B, L, _ = u.shape
    z = u @ w_in
    z_attn, z_rwkv, z_hgrn, z_gate = jnp.split(z, GROUP_OFFSETS, axis=-1)
    q_w = ATTN_HEADS * ATTN_HEAD_DIM
    kv_w = ATTN_KV_HEADS * ATTN_HEAD_DIM
    q, k, v = jnp.split(z_attn, [q_w, q_w + kv_w], axis=-1)
    y_a = sliding_window_attention(q, k, v, attn_sinks).astype(jnp.float32)
    y_b = rwkv7_mix(z_rwkv, rwkv_mu, rwkv_w0, rwkv_w_up, rwkv_a0, rwkv_a_up, rwkv_g_up,
                    rwkv_k_k, rwkv_k_a, rwkv_r_k, rwkv_ln_w, rwkv_ln_b)
    y_c = hgrn2_mix(z_hgrn, lb, hgrn_norm)
    ys = jnp.stack([y_a, y_b, y_c], axis=2).astype(u.dtype)
    up = jnp.einsum('blnw,nwd->blnd', ys, w_branch)
    gates = jax.nn.sigmoid(z_gate).reshape(B, L, N_BRANCHES, D_MODEL)
    merged = jnp.sum(gates * up, axis=2)
    return merged @ w_o


def _normal(key, shape, scale):
    return jax.random.normal(key, shape, jnp.float32) * scale


def setup_inputs(seed: int = 0) -> dict:
    key = jax.random.key(seed)
    ks = iter(jax.random.split(key, 40))
    gain = lambda shape: 1.0 + _normal(next(ks), shape, 0.05)
    Dp = DEPTH
    inp = {}
    inp['x'] = _normal(next(ks), (BATCH, SEQ, D_MODEL), 1.0)
    inp['p'] = _normal(next(ks), (DEPTH, BATCH, SEQ, PLE_DIM), 1.0)
    inp['ffn1_norm'] = gain((Dp, D_MODEL))
    inp['ffn1_w_gate'] = _normal(next(ks), (Dp, D_MODEL, D_FF), D_MODEL ** -0.5)
    inp['ffn1_w_up'] = _normal(next(ks), (Dp, D_MODEL, D_FF), D_MODEL ** -0.5)
    inp['ffn1_w_down'] = _normal(next(ks), (Dp, D_FF, D_MODEL), D_FF ** -0.5)
    inp['mix_norm'] = gain((Dp, D_MODEL))
    inp['w_in'] = _normal(next(ks), (Dp, D_MODEL, IN_WIDTH), D_MODEL ** -0.5)
    inp['attn_sinks'] = _normal(next(ks), (Dp, ATTN_HEADS), 0.5)
    inp['rwkv_mu'] = jax.random.uniform(next(ks), (Dp, RWKV_IN), jnp.float32)
    inp['rwkv_w0'] = _normal(next(ks), (Dp, BRANCH_WIDTH), 0.5)
    inp['rwkv_w_up'] = _normal(next(ks), (Dp, RWKV_W_RANK, BRANCH_WIDTH), RWKV_W_RANK ** -0.5)
    inp['rwkv_a0'] = _normal(next(ks), (Dp, BRANCH_WIDTH), 0.1)
    inp['rwkv_a_up'] = _normal(next(ks), (Dp, RWKV_A_RANK, BRANCH_WIDTH), RWKV_A_RANK ** -0.5)
    inp['rwkv_g_up'] = _normal(next(ks), (Dp, RWKV_G_RANK, BRANCH_WIDTH), RWKV_G_RANK ** -0.5)
    inp['rwkv_k_k'] = 0.85 + _normal(next(ks), (Dp, BRANCH_WIDTH), 0.05)
    inp['rwkv_k_a'] = gain((Dp, BRANCH_WIDTH))
    inp['rwkv_r_k'] = _normal(next(ks), (Dp, RWKV_HEADS, RWKV_HEAD_DIM), 0.1)
    inp['rwkv_ln_w'] = gain((Dp, BRANCH_WIDTH))
    inp['rwkv_ln_b'] = _normal(next(ks), (Dp, BRANCH_WIDTH), 0.01)
    inp['hgrn_lb'] = _normal(next(ks), (Dp, HGRN_HEADS * HGRN_KEY_DIM), 1.0)
    inp['hgrn_norm'] = gain((Dp, HGRN_VAL_DIM))
    inp['w_branch'] = _normal(next(ks), (Dp, N_BRANCHES, BRANCH_WIDTH, D_MODEL), BRANCH_WIDTH ** -0.5)
    inp['w_o'] = _normal(next(ks), (Dp, D_MODEL, D_MODEL), D_MODEL ** -0.5)
    inp['ffn2_norm'] = gain((Dp, D_MODEL))
    inp['ffn2_w_gate'] = _normal(next(ks), (Dp, D_MODEL, D_FF), D_MODEL ** -0.5)
    inp['ffn2_w_up'] = _normal(next(ks), (Dp, D_MODEL, D_FF), D_MODEL ** -0.5)
    inp['ffn2_w_down'] = _normal(next(ks), (Dp, D_FF, D_MODEL), D_FF ** -0.5)
    inp['ple_norm'] = gain((Dp, D_MODEL))
    inp['ple_w_gate'] = _normal(next(ks), (Dp, D_MODEL, D_MODEL), D_MODEL ** -0.5)
    inp['ple_w_proj'] = _normal(next(ks), (Dp, PLE_DIM, D_MODEL), PLE_DIM ** -0.5)
    inp['final_norm'] = gain((D_MODEL,))
    return inp


def reference(x, p, ffn1_norm, ffn1_w_gate, ffn1_w_up, ffn1_w_down, mix_norm, w_in, attn_sinks,
              rwkv_mu, rwkv_w0, rwkv_w_up, rwkv_a0, rwkv_a_up, rwkv_g_up, rwkv_k_k, rwkv_k_a, rwkv_r_k,
              rwkv_ln_w, rwkv_ln_b, hgrn_lb, hgrn_norm, w_branch, w_o, ffn2_norm, ffn2_w_gate, ffn2_w_up,
              ffn2_w_down, ple_norm, ple_w_gate, ple_w_proj, final_norm):
    lb_all = jax.nn.softmax(hgrn_lb.astype(jnp.float32), axis=0)
    lb_layers = jnp.clip(jnp.cumsum(lb_all, axis=0) - lb_all[0:1], 0.0, 1.0 - 1e-6)
    for i in range(DEPTH):
        h = rmsnorm(x, ffn1_norm[i])
        x = x + FFN_HALF * swiglu(h, ffn1_w_gate[i], ffn1_w_up[i], ffn1_w_down[i])
        u = rmsnorm(x, mix_norm[i])
        x = x + token_mixer(u, w_in[i], attn_sinks[i], rwkv_mu[i], rwkv_w0[i], rwkv_w_up[i], rwkv_a0[i],
                            rwkv_a_up[i], rwkv_g_up[i], rwkv_k_k[i], rwkv_k_a[i], rwkv_r_k[i], rwkv_ln_w[i],
                            rwkv_ln_b[i], lb_layers[i], hgrn_norm[i], w_branch[i], w_o[i]).astype(x.dtype)
        h = rmsnorm(x, ffn2_norm[i])
        x = x + FFN_HALF * swiglu(h, ffn2_w_gate[i], ffn2_w_up[i], ffn2_w_down[i])
        gate = jax.nn.sigmoid(rmsnorm(x, ple_norm[i]) @ ple_w_gate[i])
        x = x + gate * (p[i] @ ple_w_proj[i])
    return rmsnorm(x, final_norm)
```

```python
import functools

import jax
import jax.numpy as jnp
from jax import lax
from jax.experimental import pallas as pl
from jax.experimental.pallas import tpu as pltpu

F32 = jnp.float32
BF16 = jnp.bfloat16

NORM_EPS = 1e-6
FFN_HALF = 0.5
MASK_VALUE = -1e30
HEAD_DIM = 64
ATTN_HEADS = 8
ATTN_BLOCK = 128
ATTN_SCALE = HEAD_DIM ** -0.5
BRANCH_WIDTH = 512
RWKV_GN_EPS = 64e-5
RWKV_DECAY_SCALE = 0.6065306597126334
HGRN_HEADS = 4
HGRN_DIM = 128

LANES = 128
VMEM_LIMIT_BYTES = 56 * 1024 * 1024

FFN_ROWS = 1024
FFN_COLS = 256
ROW_TILE = 512
RWKV_CHUNK = 32
RWKV_GROUP_HEADS = 4
RWKV_GROUP_LANES = RWKV_GROUP_HEADS * HEAD_DIM
HGRN_CHUNK = 16


def _params(semantics):
    return pltpu.CompilerParams(dimension_semantics=semantics, vmem_limit_bytes=VMEM_LIMIT_BYTES)


def _mm(a, b):
    return jnp.dot(a.astype(BF16), b.astype(BF16), preferred_element_type=F32)


def _mm_nt(a, b):
    return lax.dot_general(a.astype(BF16), b.astype(BF16), (((1,), (1,)), ((), ())), preferred_element_type=F32)


def _mm_tn(a, b):
    return lax.dot_general(a.astype(BF16), b.astype(BF16), (((0,), (0,)), ((), ())), preferred_element_type=F32)


def _split3(x):
    hi = x.astype(BF16)
    r1 = x - hi.astype(F32)
    mid = r1.astype(BF16)
    lo = (r1 - mid.astype(F32)).astype(BF16)
    return hi, mid, lo


def _sel_mm(sel, x):
    sel = sel.astype(BF16)
    return sum(jnp.dot(sel, p, preferred_element_type=F32) for p in _split3(x))


def _mm_sel(x, sel):
    sel = sel.astype(BF16)
    return sum(jnp.dot(p, sel, preferred_element_type=F32) for p in _split3(x))


def _rms(x, g):
    ms = jnp.mean(x * x, axis=-1, keepdims=True)
    return x * lax.rsqrt(ms + NORM_EPS) * g


def _full(shape):
    return pl.BlockSpec(shape, lambda *_: (0,) * len(shape))


def _ffn_body(x_ref, g_ref, wg_ref, wu_ref, wd_ref, o_ref, h_ref, acc_ref):
    j = pl.program_id(1)

    @pl.when(j == 0)
    def _():
        h_ref[...] = _rms(x_ref[...], g_ref[...]).astype(BF16)
        acc_ref[...] = jnp.zeros_like(acc_ref)

    h = h_ref[...]
    gate = jnp.dot(h, wg_ref[...], preferred_element_type=F32)
    up = jnp.dot(h, wu_ref[...], preferred_element_type=F32)
    act = (gate * jax.nn.sigmoid(gate) * up).astype(BF16)
    acc_ref[...] += jnp.dot(act, wd_ref[...], preferred_element_type=F32)

    @pl.when(j == pl.num_programs(1) - 1)
    def _():
        o_ref[...] = x_ref[...] + FFN_HALF * acc_ref[...]


def ffn_half_step(x, g, wg, wu, wd):
    n, d = x.shape
    f = wg.shape[1]
    tm = min(FFN_ROWS, n)
    tf = FFN_COLS
    return pl.pallas_call(
        _ffn_body,
        grid=(n // tm, f // tf),
        in_specs=[
            pl.BlockSpec((tm, d), lambda i, j: (i, 0)),
            pl.BlockSpec((1, d), lambda i, j: (0, 0)),
            pl.BlockSpec((d, tf), lambda i, j: (0, j)),
            pl.BlockSpec((d, tf), lambda i, j: (0, j)),
            pl.BlockSpec((tf, d), lambda i, j: (j, 0)),
        ],
        out_specs=pl.BlockSpec((tm, d), lambda i, j: (i, 0)),
        out_shape=jax.ShapeDtypeStruct((n, d), F32),
        scratch_shapes=[pltpu.VMEM((tm, d), BF16), pltpu.VMEM((tm, d), F32)],
        compiler_params=_params(("parallel", "arbitrary")),
        name="ffn_half_step",
    )(x, g.reshape(1, d), wg, wu, wd)


def _proj_body(x_ref, g_ref, w_ref, o_ref):
    h = _rms(x_ref[...], g_ref[...]).astype(BF16)
    o_ref[...] = jnp.dot(h, w_ref[...], preferred_element_type=F32).astype(o_ref.dtype)


def norm_project(x, g, w, out_dtype):
    n, d = x.shape
    wout = w.shape[1]
    tm = min(ROW_TILE, n)
    return pl.pallas_call(
        _proj_body,
        grid=(n // tm,),
        in_specs=[pl.BlockSpec((tm, d), lambda i: (i, 0)), _full((1, d)), _full((d, wout))],
        out_specs=pl.BlockSpec((tm, wout), lambda i: (i, 0)),
        out_shape=jax.ShapeDtypeStruct((n, wout), out_dtype),
        compiler_params=_params(("parallel",)),
        name="norm_project",
    )(x, g.reshape(1, d), w)


def _attn_body(q_ref, kc_ref, kp_ref, vc_ref, vp_ref, sink_ref, o_ref, *, blocks_per_seq):
    blk = ATTN_BLOCK
    n = pl.program_id(0)
    first = (n % blocks_per_seq == 0).astype(jnp.int32)
    kb = jnp.concatenate([kp_ref[...], kc_ref[...]], axis=0).astype(F32)
    vb = jnp.concatenate([vp_ref[...], vc_ref[...]], axis=0).astype(F32)
    lane = lax.broadcasted_iota(jnp.int32, kb.shape, 1)
    low = lane < HEAD_DIM
    kb_sw = pltpu.roll(kb, HEAD_DIM, 1)
    vb_sw = pltpu.roll(vb, HEAD_DIM, 1)

    def placed(x, x_sw, kv_head, low_half):
        src = x if (kv_head == 0) == low_half else x_sw
        return jnp.where(low if low_half else jnp.logical_not(low), src, 0.0).astype(BF16)

    qi = lax.broadcasted_iota(jnp.int32, (blk, 2 * blk), 0)
    kj = lax.broadcasted_iota(jnp.int32, (blk, 2 * blk), 1)
    rel = blk + qi - kj
    mask = (rel >= 0) & (rel < blk) & (kj >= first * blk)

    pairs = ATTN_HEADS // 2
    for p in range(pairs):
        kv_head = p // (pairs // 2)
        qp = q_ref[:, p * LANES:(p + 1) * LANES]
        out = jnp.zeros((blk, LANES), F32)
        for half in range(2):
            head = 2 * p + half
            k_ext = placed(kb, kb_sw, kv_head, half == 0)
            v_ext = placed(vb, vb_sw, kv_head, half == 0)
            s = _mm_nt(qp, k_ext) * ATTN_SCALE
            s = jnp.where(mask, s, MASK_VALUE)
            sink = sink_ref[head:head + 1, 0:1]
            m = jnp.maximum(jnp.max(s, axis=-1, keepdims=True), sink)
            e = jnp.exp(s - m)
            denom = jnp.sum(e, axis=-1, keepdims=True) + jnp.exp(sink - m)
            out = out + _mm(e / denom, v_ext)
        o_ref[:, p * LANES:(p + 1) * LANES] = out.astype(o_ref.dtype)


def sliding_window_attention(z_attn, sinks, seq_len):
    n = z_attn.shape[0]
    blk = ATTN_BLOCK
    qw = ATTN_HEADS * HEAD_DIM
    kcol = qw // LANES
    vcol = kcol + 1
    sink_rows = jnp.broadcast_to(sinks.astype(F32)[:, None], (ATTN_HEADS, LANES))
    prev = lambda i: jnp.maximum(i - 1, 0)
    return pl.pallas_call(
        functools.partial(_attn_body, blocks_per_seq=seq_len // blk),
        grid=(n // blk,),
        in_specs=[
            pl.BlockSpec((blk, qw), lambda i: (i, 0)),
            pl.BlockSpec((blk, LANES), lambda i: (i, kcol)),
            pl.BlockSpec((blk, LANES), lambda i: (prev(i), kcol)),
            pl.BlockSpec((blk, LANES), lambda i: (i, vcol)),
            pl.BlockSpec((blk, LANES), lambda i: (prev(i), vcol)),
            _full((ATTN_HEADS, LANES)),
        ],
        out_specs=pl.BlockSpec((blk, qw), lambda i: (i, 0)),
        out_shape=jax.ShapeDtypeStruct((n, qw), BF16),
        compiler_params=_params(("parallel",)),
        name="sliding_window_attention",
    )(z_attn, z_attn, z_attn, z_attn, z_attn, sink_rows)


def _head_ones(width, head):
    r = lax.broadcasted_iota(jnp.int32, (width, width), 0) // head
    c = lax.broadcasted_iota(jnp.int32, (width, width), 1) // head
    return (r == c).astype(BF16)


def _rwkv_pre_body(z_ref, zp_ref, mu_ref, w0_ref, wup_ref, a0_ref, aup_ref, gup_ref, kk_ref, ka_ref,
                   r_o, lw_o, k_o, v_o, a_o, b_o, g_o, *, tiles_per_seq):
    bw = BRANCH_WIDTH
    i = pl.program_id(0)
    z = z_ref[...]
    keep = (i % tiles_per_seq != 0).astype(F32)
    prev_row = zp_ref[7:8, :] * keep
    row = lax.broadcasted_iota(jnp.int32, z.shape, 0)
    z_prev = jnp.where(row == 0, prev_row, pltpu.roll(z, 1, 0))
    z = z + mu_ref[...] * (z_prev - z)
    r = z[:, 0:bw]
    k = z[:, bw:2 * bw]
    v = z[:, 2 * bw:3 * bw]
    x_wa = z[:, 3 * bw:3 * bw + LANES]
    x_g = z[:, 3 * bw + LANES:3 * bw + 2 * LANES]
    d = w0_ref[...] + _mm(jnp.tanh(x_wa), wup_ref[...])
    lw = -RWKV_DECAY_SCALE * jax.nn.sigmoid(d)
    a_gate = jax.nn.sigmoid(a0_ref[...] + _mm(x_wa, aup_ref[...]))
    g = _mm(jax.nn.sigmoid(x_g), gup_ref[...])
    kk = k * kk_ref[...]
    ss = _mm_sel(kk * kk, _head_ones(bw, HEAD_DIM))
    kk = kk * lax.rsqrt(jnp.maximum(ss, 1e-24))
    r_o[...] = r
    lw_o[...] = lw
    k_o[...] = k * (1.0 + (a_gate - 1.0) * ka_ref[...])
    v_o[...] = v
    a_o[...] = -kk
    b_o[...] = kk * a_gate
    g_o[...] = g


def rwkv_prepare(z, mu, w0, w_up, a0, a_up, g_up, k_k, k_a, seq_len):
    n, zw = z.shape
    bw = BRANCH_WIDTH
    tm = min(ROW_TILE, seq_len)
    half = w_up.shape[0]
    zeros = jnp.zeros((LANES - half, bw), BF16)
    wup_ext = jnp.concatenate([w_up.astype(BF16), zeros], axis=0)
    aup_ext = jnp.concatenate([zeros, a_up.astype(BF16)], axis=0)
    row = lambda t: t.reshape(1, -1).astype(F32)
    out = jax.ShapeDtypeStruct((n, bw), F32)
    tile = pl.BlockSpec((tm, bw), lambda i: (i, 0))
    return pl.pallas_call(
        functools.partial(_rwkv_pre_body, tiles_per_seq=seq_len // tm),
        grid=(n // tm,),
        in_specs=[
            pl.BlockSpec((tm, zw), lambda i: (i, 0)),
            pl.BlockSpec((8, zw), lambda i: (jnp.maximum(i * (tm // 8) - 1, 0), 0)),
            _full((1, zw)), _full((1, bw)), _full((LANES, bw)), _full((1, bw)), _full((LANES, bw)),
            _full((LANES, bw)), _full((1, bw)), _full((1, bw)),
        ],
        out_specs=[tile] * 7,
        out_shape=[out] * 7,
        compiler_params=_params(("parallel",)),
        name="rwkv_prepare",
    )(z, z, row(mu), row(w0), wup_ext, row(a0), aup_ext, g_up.astype(BF16), row(k_k), row(k_a))


def _rwkv_scan_body(r_ref, lw_ref, k_ref, v_ref, a_ref, b_ref, y_ref, s_ref):
    c = pl.program_id(0)

    @pl.when(c == 0)
    def _():
        s_ref[...] = jnp.zeros_like(s_ref)

    batch, chunk, _ = r_ref.shape
    gh, gl = RWKV_GROUP_HEADS, RWKV_GROUP_LANES
    rows = gh * chunk
    row_head = lax.broadcasted_iota(jnp.int32, (rows, gl), 0) // chunk
    lane_head = lax.broadcasted_iota(jnp.int32, (rows, gl), 1) // HEAD_DIM
    own = row_head == lane_head
    ii = lax.broadcasted_iota(jnp.int32, (rows, rows), 0)
    jj = lax.broadcasted_iota(jnp.int32, (rows, rows), 1)
    strict = jj < ii
    incl = jj <= ii
    tri = (lax.broadcasted_iota(jnp.int32, (chunk, chunk), 1)
           <= lax.broadcasted_iota(jnp.int32, (chunk, chunk), 0))

    def stack(x):
        return jnp.where(own, jnp.concatenate([x] * gh, axis=0), 0.0).astype(BF16)

    for bi in range(batch):
        for g in range(BRANCH_WIDTH // gl):
            sl = slice(g * gl, (g + 1) * gl)
            r, lw, k, v, a, b = (ref[bi, :, sl] for ref in (r_ref, lw_ref, k_ref, v_ref, a_ref, b_ref))
            st = s_ref[bi, g]
            cs = _sel_mm(tri, lw)
            tot = cs[chunk - 1:chunk, :]
            dec_in = jnp.exp(cs)
            dec_out = jnp.exp(-cs)
            dec_end = jnp.exp(tot - cs)
            a_s = stack(a * jnp.exp(cs - lw))
            r_s = stack(r * dec_in)
            b_s = stack(b * dec_out)
            k_s = stack(k * dec_out)
            v_s = stack(v)
            b_e = stack(b * dec_end)
            k_e = stack(k * dec_end)
            ar = jnp.concatenate([a_s, r_s], axis=0)
            bk = jnp.concatenate([b_s, k_s], axis=0)
            gram = _mm_nt(ar, bk)
            m_ab = jnp.where(strict, gram[:rows, :rows], 0.0)
            m_ak = jnp.where(strict, gram[:rows, rows:], 0.0)
            m_rb = jnp.where(incl, gram[rows:, :rows], 0.0)
            m_rk = jnp.where(incl, gram[rows:, rows:], 0.0)
            ars = _mm_nt(ar, st)
            x = ars[:rows] + _mm(m_ak, v_s)
            nmat = m_ab
            steps = chunk.bit_length() - 1
            for step in range(steps):
                x = x + _mm(nmat, x)
                if step + 1 < steps:
                    nmat = _mm(nmat, nmat)
            sa = x.astype(BF16)
            y_s = ars[rows:] + _mm(m_rb, sa) + _mm(m_rk, v_s)
            y = y_s[0:chunk]
            for h in range(1, gh):
                y = y + y_s[h * chunk:(h + 1) * chunk]
            y_ref[bi, :, sl] = y
            s_ref[bi, g] = st * jnp.exp(tot) + _mm_tn(sa, b_e) + _mm_tn(v_s, k_e)


def rwkv_scan(r, lw, k, v, a, b):
    bsz, seq, bw = r.shape
    chunk = RWKV_CHUNK
    blk = pl.BlockSpec((bsz, chunk, bw), lambda c: (0, c, 0))
    groups = bw // RWKV_GROUP_LANES
    return pl.pallas_call(
        _rwkv_scan_body,
        grid=(seq // chunk,),
        in_specs=[blk] * 6,
        out_specs=blk,
        out_shape=jax.ShapeDtypeStruct((bsz, seq, bw), F32),
        scratch_shapes=[pltpu.VMEM((bsz, groups, RWKV_GROUP_LANES, RWKV_GROUP_LANES), F32)],
        compiler_params=_params(("arbitrary",)),
        name="rwkv_scan",
    )(r, lw, k, v, a, b)


def _rwkv_post_body(y_ref, r_ref, k_ref, v_ref, g_ref, lnw_ref, lnb_ref, rk_ref, o_ref):
    ones = _head_ones(BRANCH_WIDTH, HEAD_DIM)
    y = y_ref[...]
    mean = _mm_sel(y, ones) * (1.0 / HEAD_DIM)
    yc = y - mean
    var = _mm_sel(yc * yc, ones) * (1.0 / HEAD_DIM)
    yn = yc * lax.rsqrt(var + RWKV_GN_EPS) * lnw_ref[...] + lnb_ref[...]
    bonus = _mm_sel(r_ref[...] * k_ref[...] * rk_ref[...], ones) * v_ref[...]
    o_ref[...] = ((yn + bonus) * g_ref[...]).astype(o_ref.dtype)


def rwkv_finish(y, r, k, v, g, ln_w, ln_b, r_k):
    n, bw = y.shape
    tm = min(ROW_TILE, n)
    tile = pl.BlockSpec((tm, bw), lambda i: (i, 0))
    row = lambda t: t.reshape(1, bw).astype(F32)
    return pl.pallas_call(
        _rwkv_post_body,
        grid=(n // tm,),
        in_specs=[tile] * 5 + [_full((1, bw))] * 3,
        out_specs=tile,
        out_shape=jax.ShapeDtypeStruct((n, bw), BF16),
        compiler_params=_params(("parallel",)),
        name="rwkv_finish",
    )(y, r, k, v, g, row(ln_w), row(ln_b), row(r_k))


def _hgrn_body(z_ref, lb_ref, ng_ref, o_ref, s_ref):
    c = pl.program_id(0)

    @pl.when(c == 0)
    def _():
        s_ref[...] = jnp.zeros_like(s_ref)

    batch, chunk, _ = z_ref.shape
    bw = BRANCH_WIDTH
    hd = HGRN_DIM
    tri = (lax.broadcasted_iota(jnp.int32, (chunk, chunk), 1)
           <= lax.broadcasted_iota(jnp.int32, (chunk, chunk), 0))
    s_i = lax.broadcasted_iota(jnp.int32, (chunk, chunk, hd), 0)
    t_i = lax.broadcasted_iota(jnp.int32, (chunk, chunk, hd), 1)
    causal = t_i >= s_i
    for bi in range(batch):
        for h in range(HGRN_HEADS):
            col = lambda part: slice(part * bw + h * hd, part * bw + (h + 1) * hd)
            zq = z_ref[bi, :, col(0)]
            zf = z_ref[bi, :, col(1)]
            val = z_ref[bi, :, col(2)]
            zo = z_ref[bi, :, col(3)]
            lb = lb_ref[:, h * hd:(h + 1) * hd]
            q = zq * jax.nn.sigmoid(zq)
            forget = lb + (1.0 - lb) * jax.nn.sigmoid(zf)
            key = 1.0 - forget
            b = _sel_mm(tri, jnp.log(forget))
            st = s_ref[bi, h]
            rel = jnp.where(causal, b[None, :, :] - b[:, None, :], MASK_VALUE)
            w = q[None, :, :] * key[:, None, :] * jnp.exp(rel)
            attn = jnp.sum(w, axis=-1, keepdims=True)
            o = jnp.sum(attn * val[:, None, :], axis=0)
            o = o + _mm_nt(q * jnp.exp(b), st)
            b_last = b[chunk - 1:chunk, :]
            s_ref[bi, h] = st * jnp.exp(b_last) + _mm_tn(val, key * jnp.exp(b_last - b))
            o = _rms(o, ng_ref[...])
            o_ref[bi, :, h * hd:(h + 1) * hd] = (o * (zo * jax.nn.sigmoid(zo))).astype(o_ref.dtype)


def hgrn_mix(z, lb, norm_g):
    bsz, seq, zw = z.shape
    bw = BRANCH_WIDTH
    chunk = HGRN_CHUNK
    return pl.pallas_call(
        _hgrn_body,
        grid=(seq // chunk,),
        in_specs=[pl.BlockSpec((bsz, chunk, zw), lambda c: (0, c, 0)), _full((1, bw)), _full((1, HGRN_DIM))],
        out_specs=pl.BlockSpec((bsz, chunk, bw), lambda c: (0, c, 0)),
        out_shape=jax.ShapeDtypeStruct((bsz, seq, bw), BF16),
        scratch_shapes=[pltpu.VMEM((bsz, HGRN_HEADS, HGRN_DIM, HGRN_DIM), F32)],
        compiler_params=_params(("arbitrary",)),
        name="hgrn_mix",
    )(z, lb.reshape(1, bw).astype(F32), norm_g.reshape(1, HGRN_DIM).astype(F32))


def _merge_body(x_ref, ya_ref, yb_ref, yc_ref, zg_ref, wb_ref, wo_ref, o_ref):
    d = x_ref.shape[1]
    merged = jnp.zeros(x_ref.shape, F32)
    for n, y_ref in enumerate((ya_ref, yb_ref, yc_ref)):
        up = jnp.dot(y_ref[...], wb_ref[n], preferred_element_type=F32)
        merged = merged + jax.nn.sigmoid(zg_ref[:, n * d:(n + 1) * d]) * up
    o_ref[...] = x_ref[...] + _mm(merged, wo_ref[...])


def merge_branches(x, ya, yb, yc, z_gate, w_branch, w_o):
    n, d = x.shape
    bw = ya.shape[1]
    tm = min(ROW_TILE, n)
    ytile = pl.BlockSpec((tm, bw), lambda i: (i, 0))
    return pl.pallas_call(
        _merge_body,
        grid=(n // tm,),
        in_specs=[pl.BlockSpec((tm, d), lambda i: (i, 0)), ytile, ytile, ytile,
                  pl.BlockSpec((tm, 3 * d), lambda i: (i, 0)), _full(w_branch.shape), _full(w_o.shape)],
        out_specs=pl.BlockSpec((tm, d), lambda i: (i, 0)),
        out_shape=jax.ShapeDtypeStruct((n, d), F32),
        compiler_params=_params(("parallel",)),
        name="merge_branches",
    )(x, ya, yb, yc, z_gate, w_branch, w_o)


def _ple_body(x_ref, p_ref, g_ref, wg_ref, wp_ref, gf_ref, o_ref, *, final):
    x = x_ref[...]
    gate = jax.nn.sigmoid(_mm(_rms(x, g_ref[...]), wg_ref[...]))
    x = x + gate * _mm(p_ref[...], wp_ref[...])
    if final:
        x = _rms(x, gf_ref[...])
    o_ref[...] = x


def ple_step(x, p, g, w_gate, w_proj, g_final, final):
    n, d = x.shape
    pw = p.shape[1]
    tm = min(ROW_TILE, n)
    return pl.pallas_call(
        functools.partial(_ple_body, final=final),
        grid=(n // tm,),
        in_specs=[pl.BlockSpec((tm, d), lambda i: (i, 0)), pl.BlockSpec((tm, pw), lambda i: (i, 0)),
                  _full((1, d)), _full((d, d)), _full((pw, d)), _full((1, d))],
        out_specs=pl.BlockSpec((tm, d), lambda i: (i, 0)),
        out_shape=jax.ShapeDtypeStruct((n, d), F32),
        compiler_params=_params(("parallel",)),
        name="ple_step",
    )(x, p, g.reshape(1, d), w_gate, w_proj, g_final.reshape(1, d))


def token_mixer(x, seq_len, mix_norm, w_in, attn_sinks, rwkv_mu, rwkv_w0, rwkv_w_up, rwkv_a0, rwkv_a_up, rwkv_g_up,
                rwkv_k_k, rwkv_k_a, rwkv_r_k, rwkv_ln_w, rwkv_ln_b, lb, hgrn_norm, w_branch, w_o):
    n, d = x.shape
    bsz = n // seq_len
    bw = BRANCH_WIDTH
    attn_w = (ATTN_HEADS + 2 * (ATTN_HEADS // 4)) * HEAD_DIM
    rwkv_w = rwkv_mu.shape[0]
    hgrn_w = 4 * bw
    o1, o2, o3 = attn_w, attn_w + rwkv_w, attn_w + rwkv_w + hgrn_w
    w_in = w_in.astype(BF16)
    z_attn = norm_project(x, mix_norm, w_in[:, :o1], BF16)
    z_rwkv = norm_project(x, mix_norm, w_in[:, o1:o2], F32)
    z_hgrn = norm_project(x, mix_norm, w_in[:, o2:o3], F32)
    z_gate = norm_project(x, mix_norm, w_in[:, o3:], F32)

    y_a = sliding_window_attention(z_attn, attn_sinks, seq_len)

    r, lw, k, v, a, b, g = rwkv_prepare(z_rwkv, rwkv_mu, rwkv_w0, rwkv_w_up, rwkv_a0, rwkv_a_up, rwkv_g_up,
                                        rwkv_k_k, rwkv_k_a, seq_len)
    seq3 = lambda t: t.reshape(bsz, seq_len, bw)
    y = rwkv_scan(seq3(r), seq3(lw), seq3(k), seq3(v), seq3(a), seq3(b)).reshape(n, bw)
    y_b = rwkv_finish(y, r, k, v, g, rwkv_ln_w, rwkv_ln_b, rwkv_r_k)

    y_c = hgrn_mix(z_hgrn.reshape(bsz, seq_len, hgrn_w), lb, hgrn_norm).reshape(n, bw)

    return merge_branches(x, y_a, y_b, y_c, z_gate, w_branch.astype(BF16), w_o.astype(BF16))


def kernel(x, p, ffn1_norm, ffn1_w_gate, ffn1_w_up, ffn1_w_down, mix_norm, w_in, attn_sinks, rwkv_mu, rwkv_w0,
           rwkv_w_up, rwkv_a0, rwkv_a_up, rwkv_g_up, rwkv_k_k, rwkv_k_a, rwkv_r_k, rwkv_ln_w, rwkv_ln_b, hgrn_lb,
           hgrn_norm, w_branch, w_o, ffn2_norm, ffn2_w_gate, ffn2_w_up, ffn2_w_down, ple_norm, ple_w_gate,
           ple_w_proj, final_norm):
    bsz, seq_len, d = x.shape
    depth = p.shape[0]
    n = bsz * seq_len
    lb_all = jax.nn.softmax(hgrn_lb.astype(F32), axis=0)
    lb_layers = jnp.clip(jnp.cumsum(lb_all, axis=0) - lb_all[0:1], 0.0, 1.0 - 1e-6)
    bf = lambda t: t.astype(BF16)
    x = x.reshape(n, d)
    for i in range(depth):
        x = ffn_half_step(x, ffn1_norm[i], bf(ffn1_w_gate[i]), bf(ffn1_w_up[i]), bf(ffn1_w_down[i]))
        x = token_mixer(x, seq_len, mix_norm[i], w_in[i], attn_sinks[i], rwkv_mu[i], rwkv_w0[i], rwkv_w_up[i],
                        rwkv_a0[i], rwkv_a_up[i], rwkv_g_up[i], rwkv_k_k[i], rwkv_k_a[i], rwkv_r_k[i],
                        rwkv_ln_w[i], rwkv_ln_b[i], lb_layers[i], hgrn_norm[i], w_branch[i], w_o[i])
        x = ffn_half_step(x, ffn2_norm[i], bf(ffn2_w_gate[i]), bf(ffn2_w_up[i]), bf(ffn2_w_down[i]))
        x = ple_step(x, p[i].reshape(n, -1), ple_norm[i], bf(ple_w_gate[i]), bf(ple_w_proj[i]), final_norm,
                     final=(i == depth - 1))
    return x.reshape(bsz, seq_len, d)
```

```python
import functools

import jax
import jax.numpy as jnp
from jax import lax
from jax.experimental import pallas as pl
from jax.experimental.pallas import tpu as pltpu

F32 = jnp.float32
BF16 = jnp.bfloat16

NORM_EPS = 1e-6
FFN_HALF = 0.5
MASK_VALUE = -1e30
HEAD_DIM = 64
ATTN_HEADS = 8
ATTN_BLOCK = 128
ATTN_SCALE = HEAD_DIM ** -0.5
BRANCH_WIDTH = 512
RWKV_GN_EPS = 64e-5
RWKV_DECAY_SCALE = 0.6065306597126334
LOG2_E = 1.4426950408889634
HGRN_HEADS = 4
HGRN_DIM = 128

LANES = 128
VMEM_LIMIT_BYTES = 56 * 1024 * 1024

FFN_ROWS = 1024
FFN_COLS = 256
ROW_TILE = 512
RWKV_CHUNK = 32
RWKV_GROUP_HEADS = 4
RWKV_GROUP_LANES = RWKV_GROUP_HEADS * HEAD_DIM
HGRN_CHUNK = 16
HGRN_BLOCK = 128
HGRN_UNROLL = 2


def _params(semantics):
    return pltpu.CompilerParams(dimension_semantics=semantics, vmem_limit_bytes=VMEM_LIMIT_BYTES)


def _mm(a, b):
    return jnp.dot(a.astype(BF16), b.astype(BF16), preferred_element_type=F32)


def _mm_nt(a, b):
    return lax.dot_general(a.astype(BF16), b.astype(BF16), (((1,), (1,)), ((), ())), preferred_element_type=F32)


def _mm_tn(a, b):
    return lax.dot_general(a.astype(BF16), b.astype(BF16), (((0,), (0,)), ((), ())), preferred_element_type=F32)


def _split3(x):
    hi = x.astype(BF16)
    r1 = x - hi.astype(F32)
    mid = r1.astype(BF16)
    lo = (r1 - mid.astype(F32)).astype(BF16)
    return hi, mid, lo


def _sel_mm(sel, x):
    sel = sel.astype(BF16)
    return sum(jnp.dot(sel, p, preferred_element_type=F32) for p in _split3(x))


def _mm_sel(x, sel):
    sel = sel.astype(BF16)
    return sum(jnp.dot(p, sel, preferred_element_type=F32) for p in _split3(x))


def _rms(x, g):
    ms = jnp.mean(x * x, axis=-1, keepdims=True)
    return x * lax.rsqrt(ms + NORM_EPS) * g


def _full(shape):
    return pl.BlockSpec(shape, lambda *_: (0,) * len(shape))


def _ffn_body(x_ref, g_ref, wg_ref, wu_ref, wd_ref, o_ref, h_ref, acc_ref):
    j = pl.program_id(1)

    @pl.when(j == 0)
    def _():
        h_ref[...] = _rms(x_ref[...], g_ref[...]).astype(BF16)
        acc_ref[...] = jnp.zeros_like(acc_ref)

    h = h_ref[...]
    gate = jnp.dot(h, wg_ref[...], preferred_element_type=F32)
    up = jnp.dot(h, wu_ref[...], preferred_element_type=F32)
    act = (gate * jax.nn.sigmoid(gate) * up).astype(BF16)
    acc_ref[...] += jnp.dot(act, wd_ref[...], preferred_element_type=F32)

    @pl.when(j == pl.num_programs(1) - 1)
    def _():
        o_ref[...] = x_ref[...] + FFN_HALF * acc_ref[...]


def ffn_half_step(x, g, wg, wu, wd):
    n, d = x.shape
    f = wg.shape[1]
    tm = min(FFN_ROWS, n)
    tf = FFN_COLS
    return pl.pallas_call(
        _ffn_body,
        grid=(n // tm, f // tf),
        in_specs=[
            pl.BlockSpec((tm, d), lambda i, j: (i, 0)),
            pl.BlockSpec((1, d), lambda i, j: (0, 0)),
            pl.BlockSpec((d, tf), lambda i, j: (0, j)),
            pl.BlockSpec((d, tf), lambda i, j: (0, j)),
            pl.BlockSpec((tf, d), lambda i, j: (j, 0)),
        ],
        out_specs=pl.BlockSpec((tm, d), lambda i, j: (i, 0)),
        out_shape=jax.ShapeDtypeStruct((n, d), F32),
        scratch_shapes=[pltpu.VMEM((tm, d), BF16), pltpu.VMEM((tm, d), F32)],
        compiler_params=_params(("parallel", "arbitrary")),
        name="ffn_half_step",
    )(x, g.reshape(1, d), wg, wu, wd)


def _proj_body(x_ref, g_ref, w_ref, o_ref):
    h = _rms(x_ref[...], g_ref[...]).astype(BF16)
    o_ref[...] = jnp.dot(h, w_ref[...], preferred_element_type=F32).astype(o_ref.dtype)


def norm_project(x, g, w, out_dtype):
    n, d = x.shape
    wout = w.shape[1]
    tm = min(ROW_TILE, n)
    return pl.pallas_call(
        _proj_body,
        grid=(n // tm,),
        in_specs=[pl.BlockSpec((tm, d), lambda i: (i, 0)), _full((1, d)), _full((d, wout))],
        out_specs=pl.BlockSpec((tm, wout), lambda i: (i, 0)),
        out_shape=jax.ShapeDtypeStruct((n, wout), out_dtype),
        compiler_params=_params(("parallel",)),
        name="norm_project",
    )(x, g.reshape(1, d), w)


def _attn_body(q_ref, kc_ref, kp_ref, vc_ref, vp_ref, sink_ref, o_ref, *, blocks_per_seq):
    blk = ATTN_BLOCK
    n = pl.program_id(0)
    first = (n % blocks_per_seq == 0).astype(jnp.int32)
    kb = jnp.concatenate([kp_ref[...], kc_ref[...]], axis=0).astype(F32)
    vb = jnp.concatenate([vp_ref[...], vc_ref[...]], axis=0).astype(F32)
    lane = lax.broadcasted_iota(jnp.int32, kb.shape, 1)
    low = lane < HEAD_DIM
    kb_sw = pltpu.roll(kb, HEAD_DIM, 1)
    vb_sw = pltpu.roll(vb, HEAD_DIM, 1)

    def placed(x, x_sw, kv_head, low_half):
        src = x if (kv_head == 0) == low_half else x_sw
        return jnp.where(low if low_half else jnp.logical_not(low), src, 0.0).astype(BF16)

    qi = lax.broadcasted_iota(jnp.int32, (blk, 2 * blk), 0)
    kj = lax.broadcasted_iota(jnp.int32, (blk, 2 * blk), 1)
    rel = blk + qi - kj
    mask = (rel >= 0) & (rel < blk) & (kj >= first * blk)

    mask2 = jnp.concatenate([mask, mask], axis=0)
    upper = lax.broadcasted_iota(jnp.int32, (2 * blk, 1), 0) < blk
    probs = [(kv, half) for kv in range(2) for half in range(2)]
    qs = [jnp.concatenate([q_ref[:, (2 * kv) * LANES:(2 * kv + 1) * LANES],
                           q_ref[:, (2 * kv + 1) * LANES:(2 * kv + 2) * LANES]], axis=0) for kv in range(2)]
    k_ext = [placed(kb, kb_sw, kv, half == 0) for kv, half in probs]
    v_ext = [placed(vb, vb_sw, kv, half == 0) for kv, half in probs]
    scores = [_mm_nt(qs[kv], k_ext[i]) for i, (kv, _) in enumerate(probs)]
    weights = []
    for i, (kv, half) in enumerate(probs):
        s = jnp.where(mask2, scores[i] * ATTN_SCALE, MASK_VALUE)
        head_a = 4 * kv + half
        sink = jnp.where(upper, sink_ref[head_a:head_a + 1, 0:1], sink_ref[head_a + 2:head_a + 3, 0:1])
        m = jnp.maximum(jnp.max(s, axis=-1, keepdims=True), sink)
        e = jnp.exp(s - m)
        denom = jnp.sum(e, axis=-1, keepdims=True) + jnp.exp(sink - m)
        weights.append((e * (1.0 / denom)).astype(BF16))
    outs = [_mm(weights[i], v_ext[i]) for i in range(len(probs))]
    for kv in range(2):
        out = outs[2 * kv] + outs[2 * kv + 1]
        o_ref[:, (2 * kv) * LANES:(2 * kv + 1) * LANES] = out[:blk].astype(o_ref.dtype)
        o_ref[:, (2 * kv + 1) * LANES:(2 * kv + 2) * LANES] = out[blk:].astype(o_ref.dtype)


def sliding_window_attention(z_attn, sinks, seq_len):
    n = z_attn.shape[0]
    blk = ATTN_BLOCK
    qw = ATTN_HEADS * HEAD_DIM
    kcol = qw // LANES
    vcol = kcol + 1
    sink_rows = jnp.broadcast_to(sinks.astype(F32)[:, None], (ATTN_HEADS, LANES))
    prev = lambda i: jnp.maximum(i - 1, 0)
    return pl.pallas_call(
        functools.partial(_attn_body, blocks_per_seq=seq_len // blk),
        grid=(n // blk,),
        in_specs=[
            pl.BlockSpec((blk, qw), lambda i: (i, 0)),
            pl.BlockSpec((blk, LANES), lambda i: (i, kcol)),
            pl.BlockSpec((blk, LANES), lambda i: (prev(i), kcol)),
            pl.BlockSpec((blk, LANES), lambda i: (i, vcol)),
            pl.BlockSpec((blk, LANES), lambda i: (prev(i), vcol)),
            _full((ATTN_HEADS, LANES)),
        ],
        out_specs=pl.BlockSpec((blk, qw), lambda i: (i, 0)),
        out_shape=jax.ShapeDtypeStruct((n, qw), BF16),
        compiler_params=_params(("parallel",)),
        name="sliding_window_attention",
    )(z_attn, z_attn, z_attn, z_attn, z_attn, sink_rows)


def _head_ones(width, head):
    r = lax.broadcasted_iota(jnp.int32, (width, width), 0) // head
    c = lax.broadcasted_iota(jnp.int32, (width, width), 1) // head
    return (r == c).astype(BF16)


def _rwkv_pre_body(z_ref, zp_ref, mu_ref, w0_ref, wup_ref, a0_ref, aup_ref, gup_ref, kk_ref, ka_ref,
                   r_o, lw_o, k_o, v_o, a_o, b_o, g_o, *, tiles_per_seq):
    bw = BRANCH_WIDTH
    i = pl.program_id(0)
    z = z_ref[...]
    keep = (i % tiles_per_seq != 0).astype(F32)
    prev_row = zp_ref[7:8, :] * keep
    row = lax.broadcasted_iota(jnp.int32, z.shape, 0)
    z_prev = jnp.where(row == 0, prev_row, pltpu.roll(z, 1, 0))
    z = z + mu_ref[...] * (z_prev - z)
    r = z[:, 0:bw]
    k = z[:, bw:2 * bw]
    v = z[:, 2 * bw:3 * bw]
    x_wa = z[:, 3 * bw:3 * bw + LANES]
    x_g = z[:, 3 * bw + LANES:3 * bw + 2 * LANES]
    d = w0_ref[...] + _mm(jnp.tanh(x_wa), wup_ref[...])
    lw = -RWKV_DECAY_SCALE * jax.nn.sigmoid(d)
    a_gate = jax.nn.sigmoid(a0_ref[...] + _mm(x_wa, aup_ref[...]))
    g = _mm(jax.nn.sigmoid(x_g), gup_ref[...])
    kk = k * kk_ref[...]
    ss = _mm_sel(kk * kk, _head_ones(bw, HEAD_DIM))
    kk = kk * lax.rsqrt(jnp.maximum(ss, 1e-24))
    r_o[...] = r
    lw_o[...] = lw
    k_o[...] = k * (1.0 + (a_gate - 1.0) * ka_ref[...])
    v_o[...] = v
    a_o[...] = -kk
    b_o[...] = kk * a_gate
    g_o[...] = g


def rwkv_prepare(z, mu, w0, w_up, a0, a_up, g_up, k_k, k_a, seq_len):
    n, zw = z.shape
    bw = BRANCH_WIDTH
    tm = min(ROW_TILE, seq_len)
    half = w_up.shape[0]
    zeros = jnp.zeros((LANES - half, bw), BF16)
    wup_ext = jnp.concatenate([w_up.astype(BF16), zeros], axis=0)
    aup_ext = jnp.concatenate([zeros, a_up.astype(BF16)], axis=0)
    row = lambda t: t.reshape(1, -1).astype(F32)
    out = jax.ShapeDtypeStruct((n, bw), F32)
    tile = pl.BlockSpec((tm, bw), lambda i: (i, 0))
    return pl.pallas_call(
        functools.partial(_rwkv_pre_body, tiles_per_seq=seq_len // tm),
        grid=(n // tm,),
        in_specs=[
            pl.BlockSpec((tm, zw), lambda i: (i, 0)),
            pl.BlockSpec((8, zw), lambda i: (jnp.maximum(i * (tm // 8) - 1, 0), 0)),
            _full((1, zw)), _full((1, bw)), _full((LANES, bw)), _full((1, bw)), _full((LANES, bw)),
            _full((LANES, bw)), _full((1, bw)), _full((1, bw)),
        ],
        out_specs=[tile] * 7,
        out_shape=[out] * 7,
        compiler_params=_params(("parallel",)),
        name="rwkv_prepare",
    )(z, z, row(mu), row(w0), wup_ext, row(a0), aup_ext, g_up.astype(BF16), row(k_k), row(k_a))


def _rwkv_scan_body(r_ref, lw_ref, k_ref, v_ref, a_ref, b_ref, y_ref, s_ref):
    c = pl.program_id(0)

    @pl.when(c == 0)
    def _():
        s_ref[...] = jnp.zeros_like(s_ref)

    batch, chunk, _ = r_ref.shape
    gh, gl = RWKV_GROUP_HEADS, RWKV_GROUP_LANES
    rows = gh * chunk
    row_head = lax.broadcasted_iota(jnp.int32, (rows, gl), 0) // chunk
    lane_head = lax.broadcasted_iota(jnp.int32, (rows, gl), 1) // HEAD_DIM
    own = row_head == lane_head
    ii = lax.broadcasted_iota(jnp.int32, (rows, rows), 0)
    jj = lax.broadcasted_iota(jnp.int32, (rows, rows), 1)
    strict = jj < ii
    incl = jj <= ii
    tri = (lax.broadcasted_iota(jnp.int32, (chunk, chunk), 1)
           <= lax.broadcasted_iota(jnp.int32, (chunk, chunk), 0))

    def stack(x):
        return jnp.where(own, jnp.concatenate([x] * gh, axis=0), 0.0).astype(BF16)

    chains = [(bi, g) for bi in range(batch) for g in range(BRANCH_WIDTH // gl)]
    nc = range(len(chains))
    sls = [slice(g * gl, (g + 1) * gl) for _, g in chains]
    sts = [s_ref[bi, g] for bi, g in chains]
    ar, bk, v_s, b_e, k_e, tot = [], [], [], [], [], []
    for (bi, _), sl in zip(chains, sls):
        r, lw, k, v, a, b = (ref[bi, :, sl] for ref in (r_ref, lw_ref, k_ref, v_ref, a_ref, b_ref))
        cs = _sel_mm(tri, lw)
        t = cs[chunk - 1:chunk, :]
        dec_out = jnp.exp(-cs)
        dec_end = jnp.exp(t - cs)
        ar.append(jnp.concatenate([stack(a * jnp.exp(cs - lw)), stack(r * jnp.exp(cs))], axis=0))
        bk.append(jnp.concatenate([stack(b * dec_out), stack(k * dec_out)], axis=0))
        v_s.append(stack(v))
        b_e.append(stack(b * dec_end))
        k_e.append(stack(k * dec_end))
        tot.append(t)
    gram = [_mm_nt(ar[i], bk[i]) for i in nc]
    ars = [_mm_nt(ar[i], sts[i]) for i in nc]
    nmat = [jnp.where(strict, gram[i][:rows, :rows], 0.0).astype(BF16) for i in nc]
    m_ak = [jnp.where(strict, gram[i][:rows, rows:], 0.0) for i in nc]
    m_rb = [jnp.where(incl, gram[i][rows:, :rows], 0.0).astype(BF16) for i in nc]
    m_rk = [jnp.where(incl, gram[i][rows:, rows:], 0.0) for i in nc]
    x = [ars[i][:rows] + _mm(m_ak[i], v_s[i]) for i in nc]
    y0 = [ars[i][rows:] + _mm(m_rk[i], v_s[i]) for i in nc]
    steps = chunk.bit_length() - 1
    for step in range(steps):
        x = [x[i] + _mm(nmat[i], x[i]) for i in nc]
        if step + 1 < steps:
            nmat = [_mm(nmat[i], nmat[i]).astype(BF16) for i in nc]
    sa = [x[i].astype(BF16) for i in nc]
    y_s = [y0[i] + _mm(m_rb[i], sa[i]) for i in nc]
    upd = [_mm_tn(sa[i], b_e[i]) + _mm_tn(v_s[i], k_e[i]) for i in nc]
    for i, (bi, g) in enumerate(chains):
        y = y_s[i][0:chunk]
        for h in range(1, gh):
            y = y + y_s[i][h * chunk:(h + 1) * chunk]
        y_ref[bi, :, sls[i]] = y
        s_ref[bi, g] = sts[i] * jnp.exp(tot[i]) + upd[i]


def rwkv_scan(r, lw, k, v, a, b):
    bsz, seq, bw = r.shape
    chunk = RWKV_CHUNK
    blk = pl.BlockSpec((bsz, chunk, bw), lambda c: (0, c, 0))
    groups = bw // RWKV_GROUP_LANES
    return pl.pallas_call(
        _rwkv_scan_body,
        grid=(seq // chunk,),
        in_specs=[blk] * 6,
        out_specs=blk,
        out_shape=jax.ShapeDtypeStruct((bsz, seq, bw), F32),
        scratch_shapes=[pltpu.VMEM((bsz, groups, RWKV_GROUP_LANES, RWKV_GROUP_LANES), F32)],
        compiler_params=_params(("arbitrary",)),
        name="rwkv_scan",
    )(r, lw, k, v, a, b)


def _rwkv_post_body(y_ref, r_ref, k_ref, v_ref, g_ref, lnw_ref, lnb_ref, rk_ref, o_ref):
    ones = _head_ones(BRANCH_WIDTH, HEAD_DIM)
    y = y_ref[...]
    mean = _mm_sel(y, ones) * (1.0 / HEAD_DIM)
    yc = y - mean
    var = _mm_sel(yc * yc, ones) * (1.0 / HEAD_DIM)
    yn = yc * lax.rsqrt(var + RWKV_GN_EPS) * lnw_ref[...] + lnb_ref[...]
    bonus = _mm_sel(r_ref[...] * k_ref[...] * rk_ref[...], ones) * v_ref[...]
    o_ref[...] = ((yn + bonus) * g_ref[...]).astype(o_ref.dtype)


def rwkv_finish(y, r, k, v, g, ln_w, ln_b, r_k):
    n, bw = y.shape
    tm = min(ROW_TILE, n)
    tile = pl.BlockSpec((tm, bw), lambda i: (i, 0))
    row = lambda t: t.reshape(1, bw).astype(F32)
    return pl.pallas_call(
        _rwkv_post_body,
        grid=(n // tm,),
        in_specs=[tile] * 5 + [_full((1, bw))] * 3,
        out_specs=tile,
        out_shape=jax.ShapeDtypeStruct((n, bw), BF16),
        compiler_params=_params(("parallel",)),
        name="rwkv_finish",
    )(y, r, k, v, g, row(ln_w), row(ln_b), row(r_k))


def _hgrn_body(z_ref, lb_ref, ng_ref, o_ref, s_ref, c_ref):
    c = pl.program_id(0)

    @pl.when(c == 0)
    def _():
        s_ref[...] = jnp.zeros_like(s_ref)

    batch, rows, _ = z_ref.shape
    bw = BRANCH_WIDTH
    hd = HGRN_DIM
    ch = HGRN_CHUNK
    half = ch // 2
    upto = (lax.broadcasted_iota(jnp.int32, (ch, ch), 1) <= lax.broadcasted_iota(jnp.int32, (ch, ch), 0))
    row8 = lax.broadcasted_iota(jnp.int32, (half, hd), 0)
    lane8 = lax.broadcasted_iota(jnp.int32, (half, hd), 1)
    chains = [(bi, h) for bi in range(batch) for h in range(HGRN_HEADS)]

    def gates(rs, slot):
        lb = lb_ref[...]
        q, qe, ke, gam, og, b2, val = [], [], [], [], [], [], []
        for bi in range(batch):
            zq = z_ref[bi, rs, 0:bw]
            zf = z_ref[bi, rs, bw:2 * bw]
            zo = z_ref[bi, rs, 3 * bw:4 * bw]
            forget = lb + (1.0 - lb) * jax.nn.sigmoid(zf)
            key = 1.0 - forget
            b = _sel_mm(upto, jnp.log(forget))
            tot = b[ch - 1:ch, :]
            qb = zq * jax.nn.sigmoid(zq)
            q.append(qb)
            qe.append((qb * jnp.exp(b)).astype(BF16))
            ke.append((key * jnp.exp(tot - b)).astype(BF16))
            gam.append(jnp.exp(tot))
            og.append(zo * jax.nn.sigmoid(zo))
            val.append(z_ref[bi, rs, 2 * bw:3 * bw].astype(BF16))
            b2.append(b * LOG2_E)
            c_ref[slot, bi] = (b - jnp.log(key)) * LOG2_E
        return q, qe, ke, gam, og, b2, val

    def chunk_weights(q, b2, slot):
        attn = []
        for bi, h in chains:
            hs = slice(h * hd, (h + 1) * hd)
            b_lo = b2[bi][0:half, hs]
            b_hi = b2[bi][half:ch, hs]
            q_lo = q[bi][0:half, hs]
            q_hi = q[bi][half:ch, hs]
            a_lo = jnp.zeros((half, hd), F32)
            a_hi = jnp.zeros((half, hd), F32)
            for s in range(ch):
                c_s = c_ref[slot, bi, s:s + 1, hs]
                if s < half:
                    w = q_lo * jnp.exp2(jnp.where(row8 >= s, b_lo - c_s, MASK_VALUE))
                    a_lo = jnp.where(lane8 == s, jnp.sum(w, axis=-1, keepdims=True), a_lo)
                    w = q_hi * jnp.exp2(b_hi - c_s)
                else:
                    w = q_hi * jnp.exp2(jnp.where(row8 >= s - half, b_hi - c_s, MASK_VALUE))
                a_hi = jnp.where(lane8 == s, jnp.sum(w, axis=-1, keepdims=True), a_hi)
            attn.append(jnp.concatenate([a_lo, a_hi], axis=0)[:, 0:ch].astype(BF16))
        return attn

    def chunk_group(jj, carry):
        slots = range(HGRN_UNROLL)
        rss = [pl.ds(pl.multiple_of((jj * HGRN_UNROLL + u) * ch, ch), ch) for u in slots]
        gated = [gates(rss[u], u) for u in slots]
        attn = [chunk_weights(gated[u][0], gated[u][5], u) for u in slots]
        sts = [s_ref[bi, h] for bi, h in chains]
        outs = []
        for u in slots:
            _, qe, ke, gam, og, _, val = gated[u]
            hsl = [slice(h * hd, (h + 1) * hd) for _, h in chains]
            inter = [_mm_nt(qe[bi][:, hsl[i]], sts[i]) for i, (bi, _) in enumerate(chains)]
            upd = [_mm_tn(val[bi][:, hsl[i]], ke[bi][:, hsl[i]]) for i, (bi, _) in enumerate(chains)]
            intra = [_mm(attn[u][i], val[bi][:, hsl[i]]) for i, (bi, _) in enumerate(chains)]
            outs.append([_rms(inter[i] + intra[i], ng_ref[...]) * og[bi][:, hsl[i]]
                         for i, (bi, _) in enumerate(chains)])
            sts = [sts[i] * gam[bi][:, hsl[i]] + upd[i] for i, (bi, _) in enumerate(chains)]
        for u in slots:
            for i, (bi, h) in enumerate(chains):
                o_ref[bi, rss[u], h * hd:(h + 1) * hd] = outs[u][i].astype(o_ref.dtype)
        for i, (bi, h) in enumerate(chains):
            s_ref[bi, h] = sts[i]
        return carry

    lax.fori_loop(0, rows // (ch * HGRN_UNROLL), chunk_group, 0)


def hgrn_mix(z, lb, norm_g):
    bsz, seq, zw = z.shape
    bw = BRANCH_WIDTH
    rows = min(HGRN_BLOCK, seq)
    blk = pltpu.VMEM((HGRN_UNROLL, bsz, HGRN_CHUNK, bw), F32)
    return pl.pallas_call(
        _hgrn_body,
        grid=(seq // rows,),
        in_specs=[pl.BlockSpec((bsz, rows, zw), lambda c: (0, c, 0)), _full((1, bw)), _full((1, HGRN_DIM))],
        out_specs=pl.BlockSpec((bsz, rows, bw), lambda c: (0, c, 0)),
        out_shape=jax.ShapeDtypeStruct((bsz, seq, bw), BF16),
        scratch_shapes=[pltpu.VMEM((bsz, HGRN_HEADS, HGRN_DIM, HGRN_DIM), F32), blk],
        compiler_params=_params(("arbitrary",)),
        name="hgrn_mix",
    )(z, lb.reshape(1, bw).astype(F32), norm_g.reshape(1, HGRN_DIM).astype(F32))


def _merge_body(x_ref, ya_ref, yb_ref, yc_ref, zg_ref, wb_ref, wo_ref, o_ref):
    d = x_ref.shape[1]
    merged = jnp.zeros(x_ref.shape, F32)
    for n, y_ref in enumerate((ya_ref, yb_ref, yc_ref)):
        up = jnp.dot(y_ref[...], wb_ref[n], preferred_element_type=F32)
        merged = merged + jax.nn.sigmoid(zg_ref[:, n * d:(n + 1) * d]) * up
    o_ref[...] = x_ref[...] + _mm(merged, wo_ref[...])


def merge_branches(x, ya, yb, yc, z_gate, w_branch, w_o):
    n, d = x.shape
    bw = ya.shape[1]
    tm = min(ROW_TILE, n)
    ytile = pl.BlockSpec((tm, bw), lambda i: (i, 0))
    return pl.pallas_call(
        _merge_body,
        grid=(n // tm,),
        in_specs=[pl.BlockSpec((tm, d), lambda i: (i, 0)), ytile, ytile, ytile,
                  pl.BlockSpec((tm, 3 * d), lambda i: (i, 0)), _full(w_branch.shape), _full(w_o.shape)],
        out_specs=pl.BlockSpec((tm, d), lambda i: (i, 0)),
        out_shape=jax.ShapeDtypeStruct((n, d), F32),
        compiler_params=_params(("parallel",)),
        name="merge_branches",
    )(x, ya, yb, yc, z_gate, w_branch, w_o)


def _ple_body(x_ref, p_ref, g_ref, wg_ref, wp_ref, gf_ref, o_ref, *, final):
    x = x_ref[...]
    gate = jax.nn.sigmoid(_mm(_rms(x, g_ref[...]), wg_ref[...]))
    x = x + gate * _mm(p_ref[...], wp_ref[...])
    if final:
        x = _rms(x, gf_ref[...])
    o_ref[...] = x


def ple_step(x, p, g, w_gate, w_proj, g_final, final):
    n, d = x.shape
    pw = p.shape[1]
    tm = min(ROW_TILE, n)
    return pl.pallas_call(
        functools.partial(_ple_body, final=final),
        grid=(n // tm,),
        in_specs=[pl.BlockSpec((tm, d), lambda i: (i, 0)), pl.BlockSpec((tm, pw), lambda i: (i, 0)),
                  _full((1, d)), _full((d, d)), _full((pw, d)), _full((1, d))],
        out_specs=pl.BlockSpec((tm, d), lambda i: (i, 0)),
        out_shape=jax.ShapeDtypeStruct((n, d), F32),
        compiler_params=_params(("parallel",)),
        name="ple_step",
    )(x, p, g.reshape(1, d), w_gate, w_proj, g_final.reshape(1, d))


def token_mixer(x, seq_len, mix_norm, w_in, attn_sinks, rwkv_mu, rwkv_w0, rwkv_w_up, rwkv_a0, rwkv_a_up, rwkv_g_up,
                rwkv_k_k, rwkv_k_a, rwkv_r_k, rwkv_ln_w, rwkv_ln_b, lb, hgrn_norm, w_branch, w_o):
    n, d = x.shape
    bsz = n // seq_len
    bw = BRANCH_WIDTH
    attn_w = (ATTN_HEADS + 2 * (ATTN_HEADS // 4)) * HEAD_DIM
    rwkv_w = rwkv_mu.shape[0]
    hgrn_w = 4 * bw
    o1, o2, o3 = attn_w, attn_w + rwkv_w, attn_w + rwkv_w + hgrn_w
    w_in = w_in.astype(BF16)
    z_attn = norm_project(x, mix_norm, w_in[:, :o1], BF16)
    z_rwkv = norm_project(x, mix_norm, w_in[:, o1:o2], F32)
    z_hgrn = norm_project(x, mix_norm, w_in[:, o2:o3], F32)
    z_gate = norm_project(x, mix_norm, w_in[:, o3:], F32)

    y_a = sliding_window_attention(z_attn, attn_sinks, seq_len)

    r, lw, k, v, a, b, g = rwkv_prepare(z_rwkv, rwkv_mu, rwkv_w0, rwkv_w_up, rwkv_a0, rwkv_a_up, rwkv_g_up,
                                        rwkv_k_k, rwkv_k_a, seq_len)
    seq3 = lambda t: t.reshape(bsz, seq_len, bw)
    y = rwkv_scan(seq3(r), seq3(lw), seq3(k), seq3(v), seq3(a), seq3(b)).reshape(n, bw)
    y_b = rwkv_finish(y, r, k, v, g, rwkv_ln_w, rwkv_ln_b, rwkv_r_k)

    y_c = hgrn_mix(z_hgrn.reshape(bsz, seq_len, hgrn_w), lb, hgrn_norm).reshape(n, bw)

    return merge_branches(x, y_a, y_b, y_c, z_gate, w_branch.astype(BF16), w_o.astype(BF16))


def kernel(x, p, ffn1_norm, ffn1_w_gate, ffn1_w_up, ffn1_w_down, mix_norm, w_in, attn_sinks, rwkv_mu, rwkv_w0,
           rwkv_w_up, rwkv_a0, rwkv_a_up, rwkv_g_up, rwkv_k_k, rwkv_k_a, rwkv_r_k, rwkv_ln_w, rwkv_ln_b, hgrn_lb,
           hgrn_norm, w_branch, w_o, ffn2_norm, ffn2_w_gate, ffn2_w_up, ffn2_w_down, ple_norm, ple_w_gate,
           ple_w_proj, final_norm):
    bsz, seq_len, d = x.shape
    depth = p.shape[0]
    n = bsz * seq_len
    lb_all = jax.nn.softmax(hgrn_lb.astype(F32), axis=0)
    lb_layers = jnp.clip(jnp.cumsum(lb_all, axis=0) - lb_all[0:1], 0.0, 1.0 - 1e-6)
    bf = lambda t: t.astype(BF16)
    x = x.reshape(n, d)
    for i in range(depth):
        x = ffn_half_step(x, ffn1_norm[i], bf(ffn1_w_gate[i]), bf(ffn1_w_up[i]), bf(ffn1_w_down[i]))
        x = token_mixer(x, seq_len, mix_norm[i], w_in[i], attn_sinks[i], rwkv_mu[i], rwkv_w0[i], rwkv_w_up[i],
                        rwkv_a0[i], rwkv_a_up[i], rwkv_g_up[i], rwkv_k_k[i], rwkv_k_a[i], rwkv_r_k[i],
                        rwkv_ln_w[i], rwkv_ln_b[i], lb_layers[i], hgrn_norm[i], w_branch[i], w_o[i])
        x = ffn_half_step(x, ffn2_norm[i], bf(ffn2_w_gate[i]), bf(ffn2_w_up[i]), bf(ffn2_w_down[i]))
        x = ple_step(x, p[i].reshape(n, -1), ple_norm[i], bf(ple_w_gate[i]), bf(ple_w_proj[i]), final_norm,
                     final=(i == depth - 1))
    return x.reshape(bsz, seq_len, d)
```

```python
import functools

import jax
import jax.numpy as jnp
from jax import lax
from jax.experimental import pallas as pl
from jax.experimental.pallas import tpu as pltpu

F32 = jnp.float32
BF16 = jnp.bfloat16

NORM_EPS = 1e-6
FFN_HALF = 0.5
MASK_VALUE = -1e30
HEAD_DIM = 64
ATTN_HEADS = 8
ATTN_BLOCK = 128
ATTN_SCALE = HEAD_DIM ** -0.5
BRANCH_WIDTH = 512
RWKV_GN_EPS = 64e-5
RWKV_DECAY_SCALE = 0.6065306597126334
LOG2_E = 1.4426950408889634
HGRN_HEADS = 4
HGRN_DIM = 128

LANES = 128
VMEM_LIMIT_BYTES = 56 * 1024 * 1024

FFN_ROWS = 1024
FFN_COLS = 256
ROW_TILE = 512
RWKV_CHUNK = 32
RWKV_STEP_CHUNKS = 4
RWKV_GROUP_HEADS = 4
RWKV_GROUP_LANES = RWKV_GROUP_HEADS * HEAD_DIM
HGRN_CHUNK = 16
HGRN_BLOCK = 128
HGRN_UNROLL = 8


def _params(semantics):
    return pltpu.CompilerParams(dimension_semantics=semantics, vmem_limit_bytes=VMEM_LIMIT_BYTES)


def _mm(a, b):
    return jnp.dot(a.astype(BF16), b.astype(BF16), preferred_element_type=F32)


def _mm_nt(a, b):
    return lax.dot_general(a.astype(BF16), b.astype(BF16), (((1,), (1,)), ((), ())), preferred_element_type=F32)


def _mm_tn(a, b):
    return lax.dot_general(a.astype(BF16), b.astype(BF16), (((0,), (0,)), ((), ())), preferred_element_type=F32)


def _split3(x):
    hi = x.astype(BF16)
    r1 = x - hi.astype(F32)
    mid = r1.astype(BF16)
    lo = (r1 - mid.astype(F32)).astype(BF16)
    return hi, mid, lo


def _sel_mm(sel, x):
    sel = sel.astype(BF16)
    return sum(jnp.dot(sel, p, preferred_element_type=F32) for p in _split3(x))


def _mm_sel(x, sel):
    sel = sel.astype(BF16)
    return sum(jnp.dot(p, sel, preferred_element_type=F32) for p in _split3(x))


def _rms(x, g):
    ms = jnp.mean(x * x, axis=-1, keepdims=True)
    return x * lax.rsqrt(ms + NORM_EPS) * g


def _full(shape):
    return pl.BlockSpec(shape, lambda *_: (0,) * len(shape))


def _ffn_body(x_ref, g_ref, wg_ref, wu_ref, wd_ref, o_ref, h_ref, acc_ref):
    j = pl.program_id(1)

    @pl.when(j == 0)
    def _():
        h_ref[...] = _rms(x_ref[...], g_ref[...]).astype(BF16)
        acc_ref[...] = jnp.zeros_like(acc_ref)

    h = h_ref[...]
    gate = jnp.dot(h, wg_ref[...], preferred_element_type=F32)
    up = jnp.dot(h, wu_ref[...], preferred_element_type=F32)
    act = (gate * jax.nn.sigmoid(gate) * up).astype(BF16)
    acc_ref[...] += jnp.dot(act, wd_ref[...], preferred_element_type=F32)

    @pl.when(j == pl.num_programs(1) - 1)
    def _():
        o_ref[...] = x_ref[...] + FFN_HALF * acc_ref[...]


def ffn_half_step(x, g, wg, wu, wd):
    n, d = x.shape
    f = wg.shape[1]
    tm = min(FFN_ROWS, n)
    tf = FFN_COLS
    return pl.pallas_call(
        _ffn_body,
        grid=(n // tm, f // tf),
        in_specs=[
            pl.BlockSpec((tm, d), lambda i, j: (i, 0)),
            pl.BlockSpec((1, d), lambda i, j: (0, 0)),
            pl.BlockSpec((d, tf), lambda i, j: (0, j)),
            pl.BlockSpec((d, tf), lambda i, j: (0, j)),
            pl.BlockSpec((tf, d), lambda i, j: (j, 0)),
        ],
        out_specs=pl.BlockSpec((tm, d), lambda i, j: (i, 0)),
        out_shape=jax.ShapeDtypeStruct((n, d), F32),
        scratch_shapes=[pltpu.VMEM((tm, d), BF16), pltpu.VMEM((tm, d), F32)],
        compiler_params=_params(("parallel", "arbitrary")),
        name="ffn_half_step",
    )(x, g.reshape(1, d), wg, wu, wd)


def _proj_body(x_ref, g_ref, w_ref, o_ref):
    h = _rms(x_ref[...], g_ref[...]).astype(BF16)
    o_ref[...] = jnp.dot(h, w_ref[...], preferred_element_type=F32).astype(o_ref.dtype)


def norm_project(x, g, w, out_dtype):
    n, d = x.shape
    wout = w.shape[1]
    tm = min(ROW_TILE, n)
    return pl.pallas_call(
        _proj_body,
        grid=(n // tm,),
        in_specs=[pl.BlockSpec((tm, d), lambda i: (i, 0)), _full((1, d)), _full((d, wout))],
        out_specs=pl.BlockSpec((tm, wout), lambda i: (i, 0)),
        out_shape=jax.ShapeDtypeStruct((n, wout), out_dtype),
        compiler_params=_params(("parallel",)),
        name="norm_project",
    )(x, g.reshape(1, d), w)


def _attn_body(q_ref, kc_ref, kp_ref, vc_ref, vp_ref, sink_ref, o_ref, *, blocks_per_seq):
    blk = ATTN_BLOCK
    n = pl.program_id(0)
    first = (n % blocks_per_seq == 0).astype(jnp.int32)
    kb = jnp.concatenate([kp_ref[...], kc_ref[...]], axis=0).astype(F32)
    vb = jnp.concatenate([vp_ref[...], vc_ref[...]], axis=0).astype(F32)
    lane = lax.broadcasted_iota(jnp.int32, kb.shape, 1)
    low = lane < HEAD_DIM
    kb_sw = pltpu.roll(kb, HEAD_DIM, 1)
    vb_sw = pltpu.roll(vb, HEAD_DIM, 1)

    def placed(x, x_sw, kv_head, low_half):
        src = x if (kv_head == 0) == low_half else x_sw
        return jnp.where(low if low_half else jnp.logical_not(low), src, 0.0).astype(BF16)

    qi = lax.broadcasted_iota(jnp.int32, (blk, 2 * blk), 0)
    kj = lax.broadcasted_iota(jnp.int32, (blk, 2 * blk), 1)
    rel = blk + qi - kj
    mask = (rel >= 0) & (rel < blk) & (kj >= first * blk)

    probs = [(kv, half) for kv in range(2) for half in range(2)]
    qs = [jnp.concatenate([q_ref[:, (2 * kv) * LANES:(2 * kv + 1) * LANES],
                           q_ref[:, (2 * kv + 1) * LANES:(2 * kv + 2) * LANES]], axis=0) for kv in range(2)]
    k_ext = [placed(kb, kb_sw, kv, half == 0) for kv, half in probs]
    v_ext = [placed(vb, vb_sw, kv, half == 0) for kv, half in probs]
    scores = [_mm_nt(qs[kv], k_ext[i]) for i, (kv, _) in enumerate(probs)]
    weights = []
    for i, (kv, half) in enumerate(probs):
        parts = []
        for pair in range(2):
            head = 4 * kv + 2 * pair + half
            s = jnp.where(mask, scores[i][pair * blk:(pair + 1) * blk] * ATTN_SCALE, MASK_VALUE)
            sink = jnp.max(sink_ref[head:head + 1, :], axis=-1, keepdims=True)
            m = jnp.maximum(jnp.max(s, axis=-1, keepdims=True), sink)
            e = jnp.exp(s - m)
            denom = jnp.sum(e, axis=-1, keepdims=True) + jnp.exp(sink - m)
            parts.append((e * (1.0 / denom)).astype(BF16))
        weights.append(jnp.concatenate(parts, axis=0))
    outs = [_mm(weights[i], v_ext[i]) for i in range(len(probs))]
    for kv in range(2):
        out = outs[2 * kv] + outs[2 * kv + 1]
        o_ref[:, (2 * kv) * LANES:(2 * kv + 1) * LANES] = out[:blk].astype(o_ref.dtype)
        o_ref[:, (2 * kv + 1) * LANES:(2 * kv + 2) * LANES] = out[blk:].astype(o_ref.dtype)


def sliding_window_attention(z_attn, sinks, seq_len):
    n = z_attn.shape[0]
    blk = ATTN_BLOCK
    qw = ATTN_HEADS * HEAD_DIM
    kcol = qw // LANES
    vcol = kcol + 1
    sink_rows = jnp.broadcast_to(sinks.astype(F32)[:, None], (ATTN_HEADS, LANES))
    prev = lambda i: jnp.maximum(i - 1, 0)
    return pl.pallas_call(
        functools.partial(_attn_body, blocks_per_seq=seq_len // blk),
        grid=(n // blk,),
        in_specs=[
            pl.BlockSpec((blk, qw), lambda i: (i, 0)),
            pl.BlockSpec((blk, LANES), lambda i: (i, kcol)),
            pl.BlockSpec((blk, LANES), lambda i: (prev(i), kcol)),
            pl.BlockSpec((blk, LANES), lambda i: (i, vcol)),
            pl.BlockSpec((blk, LANES), lambda i: (prev(i), vcol)),
            _full((ATTN_HEADS, LANES)),
        ],
        out_specs=pl.BlockSpec((blk, qw), lambda i: (i, 0)),
        out_shape=jax.ShapeDtypeStruct((n, qw), BF16),
        compiler_params=_params(("parallel",)),
        name="sliding_window_attention",
    )(z_attn, z_attn, z_attn, z_attn, z_attn, sink_rows)


def _head_ones(width, head):
    r = lax.broadcasted_iota(jnp.int32, (width, width), 0) // head
    c = lax.broadcasted_iota(jnp.int32, (width, width), 1) // head
    return (r == c).astype(BF16)


def _rwkv_pre_body(z_ref, zp_ref, mu_ref, w0_ref, wup_ref, a0_ref, aup_ref, gup_ref, kk_ref, ka_ref,
                   r_o, lw_o, k_o, v_o, a_o, b_o, g_o, *, tiles_per_seq):
    bw = BRANCH_WIDTH
    i = pl.program_id(0)
    z = z_ref[...]
    keep = (i % tiles_per_seq != 0).astype(F32)
    prev_row = zp_ref[7:8, :] * keep
    row = lax.broadcasted_iota(jnp.int32, z.shape, 0)
    z_prev = jnp.where(row == 0, prev_row, pltpu.roll(z, 1, 0))
    z = z + mu_ref[...] * (z_prev - z)
    r = z[:, 0:bw]
    k = z[:, bw:2 * bw]
    v = z[:, 2 * bw:3 * bw]
    x_wa = z[:, 3 * bw:3 * bw + LANES]
    x_g = z[:, 3 * bw + LANES:3 * bw + 2 * LANES]
    d = w0_ref[...] + _mm(jnp.tanh(x_wa), wup_ref[...])
    lw = -RWKV_DECAY_SCALE * jax.nn.sigmoid(d)
    a_gate = jax.nn.sigmoid(a0_ref[...] + _mm(x_wa, aup_ref[...]))
    g = _mm(jax.nn.sigmoid(x_g), gup_ref[...])
    kk = k * kk_ref[...]
    ss = _mm_sel(kk * kk, _head_ones(bw, HEAD_DIM))
    kk = kk * lax.rsqrt(jnp.maximum(ss, 1e-24))
    r_o[...] = r
    lw_o[...] = lw
    k_o[...] = k * (1.0 + (a_gate - 1.0) * ka_ref[...])
    v_o[...] = v
    a_o[...] = -kk
    b_o[...] = kk * a_gate
    g_o[...] = g


def rwkv_prepare(z, mu, w0, w_up, a0, a_up, g_up, k_k, k_a, seq_len):
    n, zw = z.shape
    bw = BRANCH_WIDTH
    tm = min(ROW_TILE, seq_len)
    half = w_up.shape[0]
    zeros = jnp.zeros((LANES - half, bw), BF16)
    wup_ext = jnp.concatenate([w_up.astype(BF16), zeros], axis=0)
    aup_ext = jnp.concatenate([zeros, a_up.astype(BF16)], axis=0)
    row = lambda t: t.reshape(1, -1).astype(F32)
    out = jax.ShapeDtypeStruct((n, bw), F32)
    tile = pl.BlockSpec((tm, bw), lambda i: (i, 0))
    return pl.pallas_call(
        functools.partial(_rwkv_pre_body, tiles_per_seq=seq_len // tm),
        grid=(n // tm,),
        in_specs=[
            pl.BlockSpec((tm, zw), lambda i: (i, 0)),
            pl.BlockSpec((8, zw), lambda i: (jnp.maximum(i * (tm // 8) - 1, 0), 0)),
            _full((1, zw)), _full((1, bw)), _full((LANES, bw)), _full((1, bw)), _full((LANES, bw)),
            _full((LANES, bw)), _full((1, bw)), _full((1, bw)),
        ],
        out_specs=[tile] * 7,
        out_shape=[out] * 7,
        compiler_params=_params(("parallel",)),
        name="rwkv_prepare",
    )(z, z, row(mu), row(w0), wup_ext, row(a0), aup_ext, g_up.astype(BF16), row(k_k), row(k_a))


def _rwkv_scan_body(r_ref, lw_ref, k_ref, v_ref, a_ref, b_ref, y_ref, s_ref):
    c = pl.program_id(0)

    @pl.when(c == 0)
    def _():
        s_ref[...] = jnp.zeros_like(s_ref)

    batch = r_ref.shape[0]
    chunk = RWKV_CHUNK
    n_chunks = r_ref.shape[1] // chunk
    gh, gl = RWKV_GROUP_HEADS, RWKV_GROUP_LANES
    rows = gh * chunk
    row_head = lax.broadcasted_iota(jnp.int32, (rows, gl), 0) // chunk
    lane_head = lax.broadcasted_iota(jnp.int32, (rows, gl), 1) // HEAD_DIM
    own = row_head == lane_head
    ii = lax.broadcasted_iota(jnp.int32, (rows, rows), 0)
    jj = lax.broadcasted_iota(jnp.int32, (rows, rows), 1)
    strict = jj < ii
    incl = jj <= ii
    eye = (ii == jj).astype(F32)
    tri = (lax.broadcasted_iota(jnp.int32, (chunk, chunk), 1)
           <= lax.broadcasted_iota(jnp.int32, (chunk, chunk), 0))

    def stack(x):
        return jnp.where(own, jnp.concatenate([x] * gh, axis=0), 0.0).astype(BF16)

    chains = [(bi, g) for bi in range(batch) for g in range(BRANCH_WIDTH // gl)]
    probs = [(c, bi, g) for c in range(n_chunks) for bi, g in chains]
    npb = range(len(probs))
    ar, bk, v_s, be_ke, tot = [], [], [], [], []
    for c, bi, g in probs:
        rs = slice(c * chunk, (c + 1) * chunk)
        sl = slice(g * gl, (g + 1) * gl)
        r, lw, k, v, a, b = (ref[bi, rs, sl] for ref in (r_ref, lw_ref, k_ref, v_ref, a_ref, b_ref))
        cs = _sel_mm(tri, lw)
        t = cs[chunk - 1:chunk, :]
        dec_out = jnp.exp(-cs)
        dec_end = jnp.exp(t - cs)
        ar.append(jnp.concatenate([stack(a * jnp.exp(cs - lw)), stack(r * jnp.exp(cs))], axis=0))
        bk.append(jnp.concatenate([stack(b * dec_out), stack(k * dec_out)], axis=0))
        v_s.append(stack(v))
        be_ke.append(jnp.concatenate([stack(b * dec_end), stack(k * dec_end)], axis=0))
        tot.append(jnp.exp(t))
    gram = [_mm_nt(ar[i], bk[i]) for i in npb]
    nmat = [jnp.where(strict, gram[i][:rows, :rows], 0.0).astype(BF16) for i in npb]
    m_ak = [jnp.where(strict, gram[i][:rows, rows:], 0.0).astype(BF16) for i in npb]
    m_rbk = [jnp.where(jnp.concatenate([incl, incl], axis=1), gram[i][rows:, :], 0.0).astype(BF16) for i in npb]
    tinv = [eye + nmat[i].astype(F32) for i in npb]
    for _ in range(chunk.bit_length() - 2):
        nmat = [_mm(nmat[i], nmat[i]).astype(BF16) for i in npb]
        tinv = [tinv[i] + _mm(tinv[i], nmat[i]) for i in npb]
    tinv = [tinv[i].astype(BF16) for i in npb]
    x0 = [_mm(m_ak[i], v_s[i]) for i in npb]

    sts = [s_ref[bi, g] for bi, g in chains]
    ncn = len(chains)
    for c in range(n_chunks):
        ids = [c * ncn + j for j in range(ncn)]
        ars = [_mm_nt(ar[i], sts[j]) for j, i in enumerate(ids)]
        sa = [_mm(tinv[i], ars[j][:rows] + x0[i]).astype(BF16) for j, i in enumerate(ids)]
        sv = [jnp.concatenate([sa[j], v_s[i]], axis=0) for j, i in enumerate(ids)]
        y_s = [ars[j][rows:] + _mm(m_rbk[i], sv[j]) for j, i in enumerate(ids)]
        upd = [_mm_tn(sv[j], be_ke[i]) for j, i in enumerate(ids)]
        sts = [sts[j] * tot[i] + upd[j] for j, i in enumerate(ids)]
        for j, (bi, g) in enumerate(chains):
            y = y_s[j][0:chunk]
            for h in range(1, gh):
                y = y + y_s[j][h * chunk:(h + 1) * chunk]
            y_ref[bi, c * chunk:(c + 1) * chunk, g * gl:(g + 1) * gl] = y
    for j, (bi, g) in enumerate(chains):
        s_ref[bi, g] = sts[j]


def rwkv_scan(r, lw, k, v, a, b):
    bsz, seq, bw = r.shape
    span = min(RWKV_STEP_CHUNKS * RWKV_CHUNK, seq)
    blk = pl.BlockSpec((bsz, span, bw), lambda c: (0, c, 0))
    groups = bw // RWKV_GROUP_LANES
    return pl.pallas_call(
        _rwkv_scan_body,
        grid=(seq // span,),
        in_specs=[blk] * 6,
        out_specs=blk,
        out_shape=jax.ShapeDtypeStruct((bsz, seq, bw), F32),
        scratch_shapes=[pltpu.VMEM((bsz, groups, RWKV_GROUP_LANES, RWKV_GROUP_LANES), F32)],
        compiler_params=_params(("arbitrary",)),
        name="rwkv_scan",
    )(r, lw, k, v, a, b)


def _rwkv_post_body(y_ref, r_ref, k_ref, v_ref, g_ref, lnw_ref, lnb_ref, rk_ref, o_ref):
    ones = _head_ones(BRANCH_WIDTH, HEAD_DIM)
    y = y_ref[...]
    mean = _mm_sel(y, ones) * (1.0 / HEAD_DIM)
    yc = y - mean
    var = _mm_sel(yc * yc, ones) * (1.0 / HEAD_DIM)
    yn = yc * lax.rsqrt(var + RWKV_GN_EPS) * lnw_ref[...] + lnb_ref[...]
    bonus = _mm_sel(r_ref[...] * k_ref[...] * rk_ref[...], ones) * v_ref[...]
    o_ref[...] = ((yn + bonus) * g_ref[...]).astype(o_ref.dtype)


def rwkv_finish(y, r, k, v, g, ln_w, ln_b, r_k):
    n, bw = y.shape
    tm = min(ROW_TILE, n)
    tile = pl.BlockSpec((tm, bw), lambda i: (i, 0))
    row = lambda t: t.reshape(1, bw).astype(F32)
    return pl.pallas_call(
        _rwkv_post_body,
        grid=(n // tm,),
        in_specs=[tile] * 5 + [_full((1, bw))] * 3,
        out_specs=tile,
        out_shape=jax.ShapeDtypeStruct((n, bw), BF16),
        compiler_params=_params(("parallel",)),
        name="rwkv_finish",
    )(y, r, k, v, g, row(ln_w), row(ln_b), row(r_k))


def _hgrn_body(z_ref, lb_ref, ng_ref, o_ref, s_ref, c_ref):
    c = pl.program_id(0)

    @pl.when(c == 0)
    def _():
        s_ref[...] = jnp.zeros_like(s_ref)

    batch, rows, _ = z_ref.shape
    bw = BRANCH_WIDTH
    hd = HGRN_DIM
    ch = HGRN_CHUNK
    half = ch // 2
    upto = (lax.broadcasted_iota(jnp.int32, (ch, ch), 1) <= lax.broadcasted_iota(jnp.int32, (ch, ch), 0))
    row8 = lax.broadcasted_iota(jnp.int32, (half, hd), 0)
    lane8 = lax.broadcasted_iota(jnp.int32, (half, hd), 1)
    chains = [(bi, h) for bi in range(batch) for h in range(HGRN_HEADS)]

    def gates(rs, slot):
        lb = lb_ref[...]
        q, qe, ke, gam, og, b2, val = [], [], [], [], [], [], []
        for bi in range(batch):
            zq = z_ref[bi, rs, 0:bw]
            zf = z_ref[bi, rs, bw:2 * bw]
            zo = z_ref[bi, rs, 3 * bw:4 * bw]
            forget = lb + (1.0 - lb) * jax.nn.sigmoid(zf)
            key = 1.0 - forget
            b = _sel_mm(upto, jnp.log(forget))
            tot = b[ch - 1:ch, :]
            qb = zq * jax.nn.sigmoid(zq)
            q.append(qb)
            qe.append((qb * jnp.exp(b)).astype(BF16))
            ke.append((key * jnp.exp(tot - b)).astype(BF16))
            gam.append(jnp.exp(tot))
            og.append(zo * jax.nn.sigmoid(zo))
            val.append(z_ref[bi, rs, 2 * bw:3 * bw].astype(BF16))
            b2.append(b * LOG2_E)
            c_ref[slot, bi] = (b - jnp.log(key)) * LOG2_E
        return q, qe, ke, gam, og, b2, val

    def chunk_weights(q, b2, slot):
        attn = []
        for bi, h in chains:
            hs = slice(h * hd, (h + 1) * hd)
            b_lo = b2[bi][0:half, hs]
            b_hi = b2[bi][half:ch, hs]
            q_lo = q[bi][0:half, hs]
            q_hi = q[bi][half:ch, hs]
            a_lo = jnp.zeros((half, hd), F32)
            a_hi = jnp.zeros((half, hd), F32)
            for s in range(ch):
                c_s = c_ref[slot, bi, s:s + 1, hs]
                if s < half:
                    w = q_lo * jnp.exp2(jnp.where(row8 >= s, b_lo - c_s, MASK_VALUE))
                    a_lo = jnp.where(lane8 == s, jnp.sum(w, axis=-1, keepdims=True), a_lo)
                    w = q_hi * jnp.exp2(b_hi - c_s)
                else:
                    w = q_hi * jnp.exp2(jnp.where(row8 >= s - half, b_hi - c_s, MASK_VALUE))
                a_hi = jnp.where(lane8 == s, jnp.sum(w, axis=-1, keepdims=True), a_hi)
            attn.append(jnp.concatenate([a_lo, a_hi], axis=0)[:, 0:ch].astype(BF16))
        return attn

    def chunk_group(jj, carry):
        slots = range(HGRN_UNROLL)
        rss = [pl.ds(pl.multiple_of((jj * HGRN_UNROLL + u) * ch, ch), ch) for u in slots]
        gated = [gates(rss[u], u) for u in slots]
        attn = [chunk_weights(gated[u][0], gated[u][5], u) for u in slots]
        sts = [s_ref[bi, h] for bi, h in chains]
        outs = []
        for u in slots:
            _, qe, ke, gam, og, _, val = gated[u]
            hsl = [slice(h * hd, (h + 1) * hd) for _, h in chains]
            inter = [_mm_nt(qe[bi][:, hsl[i]], sts[i]) for i, (bi, _) in enumerate(chains)]
            upd = [_mm_tn(val[bi][:, hsl[i]], ke[bi][:, hsl[i]]) for i, (bi, _) in enumerate(chains)]
            intra = [_mm(attn[u][i], val[bi][:, hsl[i]]) for i, (bi, _) in enumerate(chains)]
            outs.append([_rms(inter[i] + intra[i], ng_ref[...]) * og[bi][:, hsl[i]]
                         for i, (bi, _) in enumerate(chains)])
            sts = [sts[i] * gam[bi][:, hsl[i]] + upd[i] for i, (bi, _) in enumerate(chains)]
        for u in slots:
            for i, (bi, h) in enumerate(chains):
                o_ref[bi, rss[u], h * hd:(h + 1) * hd] = outs[u][i].astype(o_ref.dtype)
        for i, (bi, h) in enumerate(chains):
            s_ref[bi, h] = sts[i]
        return carry

    lax.fori_loop(0, rows // (ch * HGRN_UNROLL), chunk_group, 0)


def hgrn_mix(z, lb, norm_g):
    bsz, seq, zw = z.shape
    bw = BRANCH_WIDTH
    rows = min(HGRN_BLOCK, seq)
    blk = pltpu.VMEM((HGRN_UNROLL, bsz, HGRN_CHUNK, bw), F32)
    return pl.pallas_call(
        _hgrn_body,
        grid=(seq // rows,),
        in_specs=[pl.BlockSpec((bsz, rows, zw), lambda c: (0, c, 0)), _full((1, bw)), _full((1, HGRN_DIM))],
        out_specs=pl.BlockSpec((bsz, rows, bw), lambda c: (0, c, 0)),
        out_shape=jax.ShapeDtypeStruct((bsz, seq, bw), BF16),
        scratch_shapes=[pltpu.VMEM((bsz, HGRN_HEADS, HGRN_DIM, HGRN_DIM), F32), blk],
        compiler_params=_params(("arbitrary",)),
        name="hgrn_mix",
    )(z, lb.reshape(1, bw).astype(F32), norm_g.reshape(1, HGRN_DIM).astype(F32))


def _merge_body(x_ref, ya_ref, yb_ref, yc_ref, zg_ref, wb_ref, wo_ref, o_ref):
    d = x_ref.shape[1]
    merged = jnp.zeros(x_ref.shape, F32)
    for n, y_ref in enumerate((ya_ref, yb_ref, yc_ref)):
        up = jnp.dot(y_ref[...], wb_ref[n], preferred_element_type=F32)
        merged = merged + jax.nn.sigmoid(zg_ref[:, n * d:(n + 1) * d]) * up
    o_ref[...] = x_ref[...] + _mm(merged, wo_ref[...])


def merge_branches(x, ya, yb, yc, z_gate, w_branch, w_o):
    n, d = x.shape
    bw = ya.shape[1]
    tm = min(ROW_TILE, n)
    ytile = pl.BlockSpec((tm, bw), lambda i: (i, 0))
    return pl.pallas_call(
        _merge_body,
        grid=(n // tm,),
        in_specs=[pl.BlockSpec((tm, d), lambda i: (i, 0)), ytile, ytile, ytile,
                  pl.BlockSpec((tm, 3 * d), lambda i: (i, 0)), _full(w_branch.shape), _full(w_o.shape)],
        out_specs=pl.BlockSpec((tm, d), lambda i: (i, 0)),
        out_shape=jax.ShapeDtypeStruct((n, d), F32),
        compiler_params=_params(("parallel",)),
        name="merge_branches",
    )(x, ya, yb, yc, z_gate, w_branch, w_o)


def _ple_body(x_ref, p_ref, g_ref, wg_ref, wp_ref, gf_ref, o_ref, *, final):
    x = x_ref[...]
    gate = jax.nn.sigmoid(_mm(_rms(x, g_ref[...]), wg_ref[...]))
    x = x + gate * _mm(p_ref[...], wp_ref[...])
    if final:
        x = _rms(x, gf_ref[...])
    o_ref[...] = x


def ple_step(x, p, g, w_gate, w_proj, g_final, final):
    n, d = x.shape
    pw = p.shape[1]
    tm = min(ROW_TILE, n)
    return pl.pallas_call(
        functools.partial(_ple_body, final=final),
        grid=(n // tm,),
        in_specs=[pl.BlockSpec((tm, d), lambda i: (i, 0)), pl.BlockSpec((tm, pw), lambda i: (i, 0)),
                  _full((1, d)), _full((d, d)), _full((pw, d)), _full((1, d))],
        out_specs=pl.BlockSpec((tm, d), lambda i: (i, 0)),
        out_shape=jax.ShapeDtypeStruct((n, d), F32),
        compiler_params=_params(("parallel",)),
        name="ple_step",
    )(x, p, g.reshape(1, d), w_gate, w_proj, g_final.reshape(1, d))


def token_mixer(x, seq_len, mix_norm, w_in, attn_sinks, rwkv_mu, rwkv_w0, rwkv_w_up, rwkv_a0, rwkv_a_up, rwkv_g_up,
                rwkv_k_k, rwkv_k_a, rwkv_r_k, rwkv_ln_w, rwkv_ln_b, lb, hgrn_norm, w_branch, w_o):
    n, d = x.shape
    bsz = n // seq_len
    bw = BRANCH_WIDTH
    attn_w = (ATTN_HEADS + 2 * (ATTN_HEADS // 4)) * HEAD_DIM
    rwkv_w = rwkv_mu.shape[0]
    hgrn_w = 4 * bw
    o1, o2, o3 = attn_w, attn_w + rwkv_w, attn_w + rwkv_w + hgrn_w
    w_in = w_in.astype(BF16)
    z_attn = norm_project(x, mix_norm, w_in[:, :o1], BF16)
    z_rwkv = norm_project(x, mix_norm, w_in[:, o1:o2], F32)
    z_hgrn = norm_project(x, mix_norm, w_in[:, o2:o3], F32)
    z_gate = norm_project(x, mix_norm, w_in[:, o3:], F32)

    y_a = sliding_window_attention(z_attn, attn_sinks, seq_len)

    r, lw, k, v, a, b, g = rwkv_prepare(z_rwkv, rwkv_mu, rwkv_w0, rwkv_w_up, rwkv_a0, rwkv_a_up, rwkv_g_up,
                                        rwkv_k_k, rwkv_k_a, seq_len)
    seq3 = lambda t: t.reshape(bsz, seq_len, bw)
    y = rwkv_scan(seq3(r), seq3(lw), seq3(k), seq3(v), seq3(a), seq3(b)).reshape(n, bw)
    y_b = rwkv_finish(y, r, k, v, g, rwkv_ln_w, rwkv_ln_b, rwkv_r_k)

    y_c = hgrn_mix(z_hgrn.reshape(bsz, seq_len, hgrn_w), lb, hgrn_norm).reshape(n, bw)

    return merge_branches(x, y_a, y_b, y_c, z_gate, w_branch.astype(BF16), w_o.astype(BF16))


def kernel(x, p, ffn1_norm, ffn1_w_gate, ffn1_w_up, ffn1_w_down, mix_norm, w_in, attn_sinks, rwkv_mu, rwkv_w0,
           rwkv_w_up, rwkv_a0, rwkv_a_up, rwkv_g_up, rwkv_k_k, rwkv_k_a, rwkv_r_k, rwkv_ln_w, rwkv_ln_b, hgrn_lb,
           hgrn_norm, w_branch, w_o, ffn2_norm, ffn2_w_gate, ffn2_w_up, ffn2_w_down, ple_norm, ple_w_gate,
           ple_w_proj, final_norm):
    bsz, seq_len, d = x.shape
    depth = p.shape[0]
    n = bsz * seq_len
    lb_all = jax.nn.softmax(hgrn_lb.astype(F32), axis=0)
    lb_layers = jnp.clip(jnp.cumsum(lb_all, axis=0) - lb_all[0:1], 0.0, 1.0 - 1e-6)
    bf = lambda t: t.astype(BF16)
    x = x.reshape(n, d)
    for i in range(depth):
        x = ffn_half_step(x, ffn1_norm[i], bf(ffn1_w_gate[i]), bf(ffn1_w_up[i]), bf(ffn1_w_down[i]))
        x = token_mixer(x, seq_len, mix_norm[i], w_in[i], attn_sinks[i], rwkv_mu[i], rwkv_w0[i], rwkv_w_up[i],
                        rwkv_a0[i], rwkv_a_up[i], rwkv_g_up[i], rwkv_k_k[i], rwkv_k_a[i], rwkv_r_k[i],
                        rwkv_ln_w[i], rwkv_ln_b[i], lb_layers[i], hgrn_norm[i], w_branch[i], w_o[i])
        x = ffn_half_step(x, ffn2_norm[i], bf(ffn2_w_gate[i]), bf(ffn2_w_up[i]), bf(ffn2_w_down[i]))
        x = ple_step(x, p[i].reshape(n, -1), ple_norm[i], bf(ple_w_gate[i]), bf(ple_w_proj[i]), final_norm,
                     final=(i == depth - 1))
    return x.reshape(bsz, seq_len, d)
```

```python
import functools

import jax
import jax.numpy as jnp
from jax import lax
from jax.experimental import pallas as pl
from jax.experimental.pallas import tpu as pltpu

F32 = jnp.float32
BF16 = jnp.bfloat16

NORM_EPS = 1e-6
FFN_HALF = 0.5
MASK_VALUE = -1e30
HEAD_DIM = 64
ATTN_HEADS = 8
ATTN_BLOCK = 128
ATTN_SCALE = HEAD_DIM ** -0.5
BRANCH_WIDTH = 512
RWKV_GN_EPS = 64e-5
RWKV_DECAY_SCALE = 0.6065306597126334
LOG2_E = 1.4426950408889634
HGRN_HEADS = 4
HGRN_DIM = 128

LANES = 128
VMEM_LIMIT_BYTES = 56 * 1024 * 1024

FFN_ROWS = 512
ROW_TILE = 512
RWKV_CHUNK = 32
RWKV_STEP_CHUNKS = 4
RWKV_GROUP_HEADS = 4
RWKV_GROUP_LANES = RWKV_GROUP_HEADS * HEAD_DIM
HGRN_CHUNK = 16
HGRN_BLOCK = 128
HGRN_UNROLL = 8


def _params(semantics):
    return pltpu.CompilerParams(dimension_semantics=semantics, vmem_limit_bytes=VMEM_LIMIT_BYTES)


def _mm(a, b):
    return jnp.dot(a.astype(BF16), b.astype(BF16), preferred_element_type=F32)


def _mm_nt(a, b):
    return lax.dot_general(a.astype(BF16), b.astype(BF16), (((1,), (1,)), ((), ())), preferred_element_type=F32)


def _mm_tn(a, b):
    return lax.dot_general(a.astype(BF16), b.astype(BF16), (((0,), (0,)), ((), ())), preferred_element_type=F32)


def _split3(x):
    hi = x.astype(BF16)
    r1 = x - hi.astype(F32)
    mid = r1.astype(BF16)
    lo = (r1 - mid.astype(F32)).astype(BF16)
    return hi, mid, lo


def _sel_mm(sel, x):
    sel = sel.astype(BF16)
    return sum(jnp.dot(sel, p, preferred_element_type=F32) for p in _split3(x))


def _rms(x, g):
    ms = jnp.mean(x * x, axis=-1, keepdims=True)
    return x * lax.rsqrt(ms + NORM_EPS) * g


def _full(shape):
    return pl.BlockSpec(shape, lambda *_: (0,) * len(shape))


def _resident(shape):
    return pl.BlockSpec(shape, lambda *_: (0,) * len(shape), pipeline_mode=pl.Buffered(1))


def _ffn_body(x_ref, g_ref, wg_ref, wu_ref, wd_ref, o_ref):
    x = x_ref[...]
    h = _rms(x, g_ref[...]).astype(BF16)
    gate = jnp.dot(h, wg_ref[...], preferred_element_type=F32)
    up = jnp.dot(h, wu_ref[...], preferred_element_type=F32)
    act = (gate * jax.nn.sigmoid(gate) * up).astype(BF16)
    o_ref[...] = x + FFN_HALF * jnp.dot(act, wd_ref[...], preferred_element_type=F32)


def ffn_half_step(x, g, wg, wu, wd):
    n, d = x.shape
    f = wg.shape[1]
    tm = min(FFN_ROWS, n)
    return pl.pallas_call(
        _ffn_body,
        grid=(n // tm,),
        in_specs=[pl.BlockSpec((tm, d), lambda i: (i, 0)), _full((1, d)),
                  _resident((d, f)), _resident((d, f)), _resident((f, d))],
        out_specs=pl.BlockSpec((tm, d), lambda i: (i, 0)),
        out_shape=jax.ShapeDtypeStruct((n, d), F32),
        compiler_params=_params(("parallel",)),
        name="ffn_half_step",
    )(x, g.reshape(1, d), wg, wu, wd)


def _proj_body(x_ref, g_ref, w_ref, o_ref):
    h = _rms(x_ref[...], g_ref[...]).astype(BF16)
    o_ref[...] = jnp.dot(h, w_ref[...], preferred_element_type=F32).astype(o_ref.dtype)


def norm_project(x, g, w, out_dtype):
    n, d = x.shape
    wout = w.shape[1]
    tm = min(ROW_TILE, n)
    return pl.pallas_call(
        _proj_body,
        grid=(n // tm,),
        in_specs=[pl.BlockSpec((tm, d), lambda i: (i, 0)), _full((1, d)), _resident((d, wout))],
        out_specs=pl.BlockSpec((tm, wout), lambda i: (i, 0)),
        out_shape=jax.ShapeDtypeStruct((n, wout), out_dtype),
        compiler_params=_params(("parallel",)),
        name="norm_project",
    )(x, g.reshape(1, d), w)


def _attn_body(q_ref, kc_ref, kp_ref, vc_ref, vp_ref, sink_ref, o_ref, *, blocks_per_seq):
    blk = ATTN_BLOCK
    n = pl.program_id(0)
    first = (n % blocks_per_seq == 0).astype(jnp.int32)
    kb = jnp.concatenate([kp_ref[...], kc_ref[...]], axis=0).astype(F32)
    vb = jnp.concatenate([vp_ref[...], vc_ref[...]], axis=0).astype(F32)
    lane = lax.broadcasted_iota(jnp.int32, kb.shape, 1)
    low = lane < HEAD_DIM
    kb_sw = pltpu.roll(kb, HEAD_DIM, 1)
    vb_sw = pltpu.roll(vb, HEAD_DIM, 1)

    def placed(x, x_sw, kv_head, low_half):
        src = x if (kv_head == 0) == low_half else x_sw
        return jnp.where(low if low_half else jnp.logical_not(low), src, 0.0).astype(BF16)

    qi = lax.broadcasted_iota(jnp.int32, (blk, 2 * blk), 0)
    kj = lax.broadcasted_iota(jnp.int32, (blk, 2 * blk), 1)
    rel = blk + qi - kj
    mask = (rel >= 0) & (rel < blk) & (kj >= first * blk)

    probs = [(kv, half) for kv in range(2) for half in range(2)]
    qs = [jnp.concatenate([q_ref[:, (2 * kv) * LANES:(2 * kv + 1) * LANES],
                           q_ref[:, (2 * kv + 1) * LANES:(2 * kv + 2) * LANES]], axis=0) for kv in range(2)]
    k_ext = [placed(kb, kb_sw, kv, half == 0) for kv, half in probs]
    v_ext = [placed(vb, vb_sw, kv, half == 0) for kv, half in probs]
    scores = [_mm_nt(qs[kv], k_ext[i]) for i, (kv, _) in enumerate(probs)]
    weights = []
    for i, (kv, half) in enumerate(probs):
        parts = []
        for pair in range(2):
            head = 4 * kv + 2 * pair + half
            s = jnp.where(mask, scores[i][pair * blk:(pair + 1) * blk] * ATTN_SCALE, MASK_VALUE)
            sink = jnp.max(sink_ref[head:head + 1, :], axis=-1, keepdims=True)
            m = jnp.maximum(jnp.max(s, axis=-1, keepdims=True), sink)
            e = jnp.exp(s - m)
            denom = jnp.sum(e, axis=-1, keepdims=True) + jnp.exp(sink - m)
            parts.append((e * (1.0 / denom)).astype(BF16))
        weights.append(jnp.concatenate(parts, axis=0))
    outs = [_mm(weights[i], v_ext[i]) for i in range(len(probs))]
    for kv in range(2):
        out = outs[2 * kv] + outs[2 * kv + 1]
        o_ref[:, (2 * kv) * LANES:(2 * kv + 1) * LANES] = out[:blk].astype(o_ref.dtype)
        o_ref[:, (2 * kv + 1) * LANES:(2 * kv + 2) * LANES] = out[blk:].astype(o_ref.dtype)


def sliding_window_attention(z_attn, sinks, seq_len):
    n = z_attn.shape[0]
    blk = ATTN_BLOCK
    qw = ATTN_HEADS * HEAD_DIM
    kcol = qw // LANES
    vcol = kcol + 1
    sink_rows = jnp.broadcast_to(sinks.astype(F32)[:, None], (ATTN_HEADS, LANES))
    prev = lambda i: jnp.maximum(i - 1, 0)
    return pl.pallas_call(
        functools.partial(_attn_body, blocks_per_seq=seq_len // blk),
        grid=(n // blk,),
        in_specs=[
            pl.BlockSpec((blk, qw), lambda i: (i, 0)),
            pl.BlockSpec((blk, LANES), lambda i: (i, kcol)),
            pl.BlockSpec((blk, LANES), lambda i: (prev(i), kcol)),
            pl.BlockSpec((blk, LANES), lambda i: (i, vcol)),
            pl.BlockSpec((blk, LANES), lambda i: (prev(i), vcol)),
            _full((ATTN_HEADS, LANES)),
        ],
        out_specs=pl.BlockSpec((blk, qw), lambda i: (i, 0)),
        out_shape=jax.ShapeDtypeStruct((n, qw), BF16),
        compiler_params=_params(("parallel",)),
        name="sliding_window_attention",
    )(z_attn, z_attn, z_attn, z_attn, z_attn, sink_rows)


def _head_ones(width, head):
    r = lax.broadcasted_iota(jnp.int32, (width, width), 0) // head
    c = lax.broadcasted_iota(jnp.int32, (width, width), 1) // head
    return (r == c).astype(BF16)


def _rwkv_pre_body(z_ref, zp_ref, mu_ref, w0_ref, wup_ref, a0_ref, aup_ref, gup_ref, kk_ref, ka_ref,
                   r_o, lw_o, k_o, v_o, a_o, b_o, g_o, *, tiles_per_seq):
    bw = BRANCH_WIDTH
    i = pl.program_id(0)
    z = z_ref[...]
    keep = (i % tiles_per_seq != 0).astype(F32)
    prev_row = zp_ref[7:8, :] * keep
    row = lax.broadcasted_iota(jnp.int32, z.shape, 0)
    z_prev = jnp.where(row == 0, prev_row, pltpu.roll(z, 1, 0))
    z = z + mu_ref[...] * (z_prev - z)
    r = z[:, 0:bw]
    k = z[:, bw:2 * bw]
    v = z[:, 2 * bw:3 * bw]
    x_wa = z[:, 3 * bw:3 * bw + LANES]
    x_g = z[:, 3 * bw + LANES:3 * bw + 2 * LANES]
    d = w0_ref[...] + _mm(jnp.tanh(x_wa), wup_ref[...])
    lw = -RWKV_DECAY_SCALE * jax.nn.sigmoid(d)
    a_gate = jax.nn.sigmoid(a0_ref[...] + _mm(x_wa, aup_ref[...]))
    g = _mm(jax.nn.sigmoid(x_g), gup_ref[...])
    kk = k * kk_ref[...]
    ss = _mm(kk * kk, _head_ones(bw, HEAD_DIM))
    kk = kk * lax.rsqrt(jnp.maximum(ss, 1e-24))
    r_o[...] = r
    lw_o[...] = lw
    k_o[...] = k * (1.0 + (a_gate - 1.0) * ka_ref[...])
    v_o[...] = v
    a_o[...] = -kk
    b_o[...] = kk * a_gate
    g_o[...] = g


def rwkv_prepare(z, mu, w0, w_up, a0, a_up, g_up, k_k, k_a, seq_len):
    n, zw = z.shape
    bw = BRANCH_WIDTH
    tm = min(ROW_TILE, seq_len)
    half = w_up.shape[0]
    zeros = jnp.zeros((LANES - half, bw), BF16)
    wup_ext = jnp.concatenate([w_up.astype(BF16), zeros], axis=0)
    aup_ext = jnp.concatenate([zeros, a_up.astype(BF16)], axis=0)
    row = lambda t: t.reshape(1, -1).astype(F32)
    out = jax.ShapeDtypeStruct((n, bw), F32)
    tile = pl.BlockSpec((tm, bw), lambda i: (i, 0))
    return pl.pallas_call(
        functools.partial(_rwkv_pre_body, tiles_per_seq=seq_len // tm),
        grid=(n // tm,),
        in_specs=[
            pl.BlockSpec((tm, zw), lambda i: (i, 0)),
            pl.BlockSpec((8, zw), lambda i: (jnp.maximum(i * (tm // 8) - 1, 0), 0)),
            _full((1, zw)), _full((1, bw)), _full((LANES, bw)), _full((1, bw)), _full((LANES, bw)),
            _full((LANES, bw)), _full((1, bw)), _full((1, bw)),
        ],
        out_specs=[tile] * 7,
        out_shape=[out] * 7,
        compiler_params=_params(("parallel",)),
        name="rwkv_prepare",
    )(z, z, row(mu), row(w0), wup_ext, row(a0), aup_ext, g_up.astype(BF16), row(k_k), row(k_a))


def _rwkv_scan_body(r_ref, lw_ref, k_ref, v_ref, a_ref, b_ref, y_ref, s_ref):
    c = pl.program_id(0)

    @pl.when(c == 0)
    def _():
        s_ref[...] = jnp.zeros_like(s_ref)

    batch = r_ref.shape[0]
    chunk = RWKV_CHUNK
    n_chunks = r_ref.shape[1] // chunk
    gh, gl = RWKV_GROUP_HEADS, RWKV_GROUP_LANES
    rows = gh * chunk
    row_head = lax.broadcasted_iota(jnp.int32, (rows, gl), 0) // chunk
    lane_head = lax.broadcasted_iota(jnp.int32, (rows, gl), 1) // HEAD_DIM
    own = row_head == lane_head
    ii = lax.broadcasted_iota(jnp.int32, (rows, rows), 0)
    jj = lax.broadcasted_iota(jnp.int32, (rows, rows), 1)
    strict = jj < ii
    incl = jj <= ii
    eye = (ii == jj).astype(F32)
    tri = (lax.broadcasted_iota(jnp.int32, (chunk, chunk), 1)
           <= lax.broadcasted_iota(jnp.int32, (chunk, chunk), 0))

    def stack(x):
        return jnp.where(own, jnp.concatenate([x] * gh, axis=0), 0.0).astype(BF16)

    chains = [(bi, g) for bi in range(batch) for g in range(BRANCH_WIDTH // gl)]
    probs = [(c, bi, g) for c in range(n_chunks) for bi, g in chains]
    npb = range(len(probs))
    ar, bk, v_s, be_ke, tot = [], [], [], [], []
    for c, bi, g in probs:
        rs = slice(c * chunk, (c + 1) * chunk)
        sl = slice(g * gl, (g + 1) * gl)
        r, lw, k, v, a, b = (ref[bi, rs, sl] for ref in (r_ref, lw_ref, k_ref, v_ref, a_ref, b_ref))
        cs = _sel_mm(tri, lw)
        t = cs[chunk - 1:chunk, :]
        dec_out = jnp.exp(-cs)
        dec_end = jnp.exp(t - cs)
        ar.append(jnp.concatenate([stack(a * jnp.exp(cs - lw)), stack(r * jnp.exp(cs))], axis=0))
        bk.append(jnp.concatenate([stack(b * dec_out), stack(k * dec_out)], axis=0))
        v_s.append(stack(v))
        be_ke.append(jnp.concatenate([stack(b * dec_end), stack(k * dec_end)], axis=0))
        tot.append(jnp.exp(t))
    gram = [_mm_nt(ar[i], bk[i]) for i in npb]
    nmat = [jnp.where(strict, gram[i][:rows, :rows], 0.0).astype(BF16) for i in npb]
    m_ak = [jnp.where(strict, gram[i][:rows, rows:], 0.0).astype(BF16) for i in npb]
    m_rbk = [jnp.where(jnp.concatenate([incl, incl], axis=1), gram[i][rows:, :], 0.0).astype(BF16) for i in npb]
    tinv = [eye + nmat[i].astype(F32) for i in npb]
    for _ in range(chunk.bit_length() - 2):
        nmat = [_mm(nmat[i], nmat[i]).astype(BF16) for i in npb]
        tinv = [tinv[i] + _mm(tinv[i], nmat[i]) for i in npb]
    tinv = [tinv[i].astype(BF16) for i in npb]
    x0 = [_mm(m_ak[i], v_s[i]) for i in npb]

    sts = [s_ref[bi, g] for bi, g in chains]
    ncn = len(chains)
    for c in range(n_chunks):
        ids = [c * ncn + j for j in range(ncn)]
        ars = [_mm_nt(ar[i], sts[j]) for j, i in enumerate(ids)]
        sa = [_mm(tinv[i], ars[j][:rows] + x0[i]).astype(BF16) for j, i in enumerate(ids)]
        sv = [jnp.concatenate([sa[j], v_s[i]], axis=0) for j, i in enumerate(ids)]
        y_s = [ars[j][rows:] + _mm(m_rbk[i], sv[j]) for j, i in enumerate(ids)]
        upd = [_mm_tn(sv[j], be_ke[i]) for j, i in enumerate(ids)]
        sts = [sts[j] * tot[i] + upd[j] for j, i in enumerate(ids)]
        for j, (bi, g) in enumerate(chains):
            y = y_s[j][0:chunk]
            for h in range(1, gh):
                y = y + y_s[j][h * chunk:(h + 1) * chunk]
            y_ref[bi, c * chunk:(c + 1) * chunk, g * gl:(g + 1) * gl] = y
    for j, (bi, g) in enumerate(chains):
        s_ref[bi, g] = sts[j]


def rwkv_scan(r, lw, k, v, a, b):
    bsz, seq, bw = r.shape
    span = min(RWKV_STEP_CHUNKS * RWKV_CHUNK, seq)
    blk = pl.BlockSpec((bsz, span, bw), lambda c: (0, c, 0))
    groups = bw // RWKV_GROUP_LANES
    return pl.pallas_call(
        _rwkv_scan_body,
        grid=(seq // span,),
        in_specs=[blk] * 6,
        out_specs=blk,
        out_shape=jax.ShapeDtypeStruct((bsz, seq, bw), F32),
        scratch_shapes=[pltpu.VMEM((bsz, groups, RWKV_GROUP_LANES, RWKV_GROUP_LANES), F32)],
        compiler_params=_params(("arbitrary",)),
        name="rwkv_scan",
    )(r, lw, k, v, a, b)


def _rwkv_post_body(y_ref, r_ref, k_ref, v_ref, g_ref, lnw_ref, lnb_ref, rk_ref, o_ref):
    ones = _head_ones(BRANCH_WIDTH, HEAD_DIM)
    y = y_ref[...]
    mean = _mm(y, ones) * (1.0 / HEAD_DIM)
    yc = y - mean
    var = _mm(yc * yc, ones) * (1.0 / HEAD_DIM)
    yn = yc * lax.rsqrt(var + RWKV_GN_EPS) * lnw_ref[...] + lnb_ref[...]
    bonus = _mm(r_ref[...] * k_ref[...] * rk_ref[...], ones) * v_ref[...]
    o_ref[...] = ((yn + bonus) * g_ref[...]).astype(o_ref.dtype)


def rwkv_finish(y, r, k, v, g, ln_w, ln_b, r_k):
    n, bw = y.shape
    tm = min(ROW_TILE, n)
    tile = pl.BlockSpec((tm, bw), lambda i: (i, 0))
    row = lambda t: t.reshape(1, bw).astype(F32)
    return pl.pallas_call(
        _rwkv_post_body,
        grid=(n // tm,),
        in_specs=[tile] * 5 + [_full((1, bw))] * 3,
        out_specs=tile,
        out_shape=jax.ShapeDtypeStruct((n, bw), BF16),
        compiler_params=_params(("parallel",)),
        name="rwkv_finish",
    )(y, r, k, v, g, row(ln_w), row(ln_b), row(r_k))


def _hgrn_body(z_ref, lb_ref, ng_ref, o_ref, s_ref, c_ref):
    c = pl.program_id(0)

    @pl.when(c == 0)
    def _():
        s_ref[...] = jnp.zeros_like(s_ref)

    batch, rows, _ = z_ref.shape
    bw = BRANCH_WIDTH
    hd = HGRN_DIM
    ch = HGRN_CHUNK
    half = ch // 2
    upto = (lax.broadcasted_iota(jnp.int32, (ch, ch), 1) <= lax.broadcasted_iota(jnp.int32, (ch, ch), 0))
    row8 = lax.broadcasted_iota(jnp.int32, (half, hd), 0)
    lane8 = lax.broadcasted_iota(jnp.int32, (half, hd), 1)
    chains = [(bi, h) for bi in range(batch) for h in range(HGRN_HEADS)]

    def gates(rs, slot):
        lb = lb_ref[...]
        q, qe, ke, gam, og, b2, val = [], [], [], [], [], [], []
        for bi in range(batch):
            zq = z_ref[bi, rs, 0:bw]
            zf = z_ref[bi, rs, bw:2 * bw]
            zo = z_ref[bi, rs, 3 * bw:4 * bw]
            forget = lb + (1.0 - lb) * jax.nn.sigmoid(zf)
            key = 1.0 - forget
            b = _sel_mm(upto, jnp.log(forget))
            tot = b[ch - 1:ch, :]
            qb = zq * jax.nn.sigmoid(zq)
            q.append(qb)
            qe.append((qb * jnp.exp(b)).astype(BF16))
            ke.append((key * jnp.exp(tot - b)).astype(BF16))
            gam.append(jnp.exp(tot))
            og.append(zo * jax.nn.sigmoid(zo))
            val.append(z_ref[bi, rs, 2 * bw:3 * bw].astype(BF16))
            b2.append(b * LOG2_E)
            c_ref[slot, bi] = (b - jnp.log(key)) * LOG2_E
        return q, qe, ke, gam, og, b2, val

    def chunk_weights(q, b2, slot):
        attn = []
        for bi, h in chains:
            hs = slice(h * hd, (h + 1) * hd)
            b_lo = b2[bi][0:half, hs]
            b_hi = b2[bi][half:ch, hs]
            q_lo = q[bi][0:half, hs]
            q_hi = q[bi][half:ch, hs]
            a_lo = jnp.zeros((half, hd), F32)
            a_hi = jnp.zeros((half, hd), F32)
            for s in range(ch):
                c_s = c_ref[slot, bi, s:s + 1, hs]
                if s < half:
                    w = q_lo * jnp.exp2(jnp.where(row8 >= s, b_lo - c_s, MASK_VALUE))
                    a_lo = jnp.where(lane8 == s, jnp.sum(w, axis=-1, keepdims=True), a_lo)
                    w = q_hi * jnp.exp2(b_hi - c_s)
                else:
                    w = q_hi * jnp.exp2(jnp.where(row8 >= s - half, b_hi - c_s, MASK_VALUE))
                a_hi = jnp.where(lane8 == s, jnp.sum(w, axis=-1, keepdims=True), a_hi)
            attn.append(jnp.concatenate([a_lo, a_hi], axis=0)[:, 0:ch].astype(BF16))
        return attn

    def chunk_group(jj, carry):
        slots = range(HGRN_UNROLL)
        rss = [pl.ds(pl.multiple_of((jj * HGRN_UNROLL + u) * ch, ch), ch) for u in slots]
        gated = [gates(rss[u], u) for u in slots]
        attn = [chunk_weights(gated[u][0], gated[u][5], u) for u in slots]
        sts = [s_ref[bi, h] for bi, h in chains]
        outs = []
        for u in slots:
            _, qe, ke, gam, og, _, val = gated[u]
            hsl = [slice(h * hd, (h + 1) * hd) for _, h in chains]
            inter = [_mm_nt(qe[bi][:, hsl[i]], sts[i]) for i, (bi, _) in enumerate(chains)]
            upd = [_mm_tn(val[bi][:, hsl[i]], ke[bi][:, hsl[i]]) for i, (bi, _) in enumerate(chains)]
            intra = [_mm(attn[u][i], val[bi][:, hsl[i]]) for i, (bi, _) in enumerate(chains)]
            outs.append([_rms(inter[i] + intra[i], ng_ref[...]) * og[bi][:, hsl[i]]
                         for i, (bi, _) in enumerate(chains)])
            sts = [sts[i] * gam[bi][:, hsl[i]] + upd[i] for i, (bi, _) in enumerate(chains)]
        for u in slots:
            for i, (bi, h) in enumerate(chains):
                o_ref[bi, rss[u], h * hd:(h + 1) * hd] = outs[u][i].astype(o_ref.dtype)
        for i, (bi, h) in enumerate(chains):
            s_ref[bi, h] = sts[i]
        return carry

    lax.fori_loop(0, rows // (ch * HGRN_UNROLL), chunk_group, 0)


def hgrn_mix(z, lb, norm_g):
    bsz, seq, zw = z.shape
    bw = BRANCH_WIDTH
    rows = min(HGRN_BLOCK, seq)
    blk = pltpu.VMEM((HGRN_UNROLL, bsz, HGRN_CHUNK, bw), F32)
    return pl.pallas_call(
        _hgrn_body,
        grid=(seq // rows,),
        in_specs=[pl.BlockSpec((bsz, rows, zw), lambda c: (0, c, 0)), _full((1, bw)), _full((1, HGRN_DIM))],
        out_specs=pl.BlockSpec((bsz, rows, bw), lambda c: (0, c, 0)),
        out_shape=jax.ShapeDtypeStruct((bsz, seq, bw), BF16),
        scratch_shapes=[pltpu.VMEM((bsz, HGRN_HEADS, HGRN_DIM, HGRN_DIM), F32), blk],
        compiler_params=_params(("arbitrary",)),
        name="hgrn_mix",
    )(z, lb.reshape(1, bw).astype(F32), norm_g.reshape(1, HGRN_DIM).astype(F32))


def _merge_body(x_ref, g_ref, ya_ref, yb_ref, yc_ref, wgate_ref, wb_ref, wo_ref, o_ref):
    x = x_ref[...]
    d = x.shape[1]
    h = _rms(x, g_ref[...]).astype(BF16)
    merged = jnp.zeros(x.shape, F32)
    for n, y_ref in enumerate((ya_ref, yb_ref, yc_ref)):
        gate = jax.nn.sigmoid(jnp.dot(h, wgate_ref[:, n * d:(n + 1) * d], preferred_element_type=F32))
        merged = merged + gate * jnp.dot(y_ref[...], wb_ref[n], preferred_element_type=F32)
    o_ref[...] = x + _mm(merged, wo_ref[...])


def merge_branches(x, g, ya, yb, yc, w_gate, w_branch, w_o):
    n, d = x.shape
    bw = ya.shape[1]
    tm = min(ROW_TILE, n)
    ytile = pl.BlockSpec((tm, bw), lambda i: (i, 0))
    return pl.pallas_call(
        _merge_body,
        grid=(n // tm,),
        in_specs=[pl.BlockSpec((tm, d), lambda i: (i, 0)), _full((1, d)), ytile, ytile, ytile,
                  _resident(w_gate.shape), _resident(w_branch.shape), _resident(w_o.shape)],
        out_specs=pl.BlockSpec((tm, d), lambda i: (i, 0)),
        out_shape=jax.ShapeDtypeStruct((n, d), F32),
        compiler_params=_params(("parallel",)),
        name="merge_branches",
    )(x, g.reshape(1, d), ya, yb, yc, w_gate, w_branch, w_o)


def _ple_body(x_ref, p_ref, g_ref, wg_ref, wp_ref, gf_ref, o_ref, *, final):
    x = x_ref[...]
    gate = jax.nn.sigmoid(_mm(_rms(x, g_ref[...]), wg_ref[...]))
    x = x + gate * _mm(p_ref[...], wp_ref[...])
    if final:
        x = _rms(x, gf_ref[...])
    o_ref[...] = x


def ple_step(x, p, g, w_gate, w_proj, g_final, final):
    n, d = x.shape
    pw = p.shape[1]
    tm = min(ROW_TILE, n)
    return pl.pallas_call(
        functools.partial(_ple_body, final=final),
        grid=(n // tm,),
        in_specs=[pl.BlockSpec((tm, d), lambda i: (i, 0)), pl.BlockSpec((tm, pw), lambda i: (i, 0)),
                  _full((1, d)), _resident((d, d)), _resident((pw, d)), _full((1, d))],
        out_specs=pl.BlockSpec((tm, d), lambda i: (i, 0)),
        out_shape=jax.ShapeDtypeStruct((n, d), F32),
        compiler_params=_params(("parallel",)),
        name="ple_step",
    )(x, p, g.reshape(1, d), w_gate, w_proj, g_final.reshape(1, d))


def token_mixer(x, seq_len, mix_norm, w_in, attn_sinks, rwkv_mu, rwkv_w0, rwkv_w_up, rwkv_a0, rwkv_a_up, rwkv_g_up,
                rwkv_k_k, rwkv_k_a, rwkv_r_k, rwkv_ln_w, rwkv_ln_b, lb, hgrn_norm, w_branch, w_o):
    n, d = x.shape
    bsz = n // seq_len
    bw = BRANCH_WIDTH
    attn_w = (ATTN_HEADS + 2 * (ATTN_HEADS // 4)) * HEAD_DIM
    rwkv_w = rwkv_mu.shape[0]
    hgrn_w = 4 * bw
    o1, o2, o3 = attn_w, attn_w + rwkv_w, attn_w + rwkv_w + hgrn_w
    w_in = w_in.astype(BF16)
    z_attn = norm_project(x, mix_norm, w_in[:, :o1], BF16)
    z_rwkv = norm_project(x, mix_norm, w_in[:, o1:o2], F32)
    z_hgrn = norm_project(x, mix_norm, w_in[:, o2:o3], F32)

    y_a = sliding_window_attention(z_attn, attn_sinks, seq_len)

    r, lw, k, v, a, b, g = rwkv_prepare(z_rwkv, rwkv_mu, rwkv_w0, rwkv_w_up, rwkv_a0, rwkv_a_up, rwkv_g_up,
                                        rwkv_k_k, rwkv_k_a, seq_len)
    seq3 = lambda t: t.reshape(bsz, seq_len, bw)
    y = rwkv_scan(seq3(r), seq3(lw), seq3(k), seq3(v), seq3(a), seq3(b)).reshape(n, bw)
    y_b = rwkv_finish(y, r, k, v, g, rwkv_ln_w, rwkv_ln_b, rwkv_r_k)

    y_c = hgrn_mix(z_hgrn.reshape(bsz, seq_len, hgrn_w), lb, hgrn_norm).reshape(n, bw)

    return merge_branches(x, mix_norm, y_a, y_b, y_c, w_in[:, o3:], w_branch.astype(BF16), w_o.astype(BF16))


def kernel(x, p, ffn1_norm, ffn1_w_gate, ffn1_w_up, ffn1_w_down, mix_norm, w_in, attn_sinks, rwkv_mu, rwkv_w0,
           rwkv_w_up, rwkv_a0, rwkv_a_up, rwkv_g_up, rwkv_k_k, rwkv_k_a, rwkv_r_k, rwkv_ln_w, rwkv_ln_b, hgrn_lb,
           hgrn_norm, w_branch, w_o, ffn2_norm, ffn2_w_gate, ffn2_w_up, ffn2_w_down, ple_norm, ple_w_gate,
           ple_w_proj, final_norm):
    bsz, seq_len, d = x.shape
    depth = p.shape[0]
    n = bsz * seq_len
    lb_all = jax.nn.softmax(hgrn_lb.astype(F32), axis=0)
    lb_layers = jnp.clip(jnp.cumsum(lb_all, axis=0) - lb_all[0:1], 0.0, 1.0 - 1e-6)
    bf = lambda t: t.astype(BF16)
    x = x.reshape(n, d)
    for i in range(depth):
        x = ffn_half_step(x, ffn1_norm[i], bf(ffn1_w_gate[i]), bf(ffn1_w_up[i]), bf(ffn1_w_down[i]))
        x = token_mixer(x, seq_len, mix_norm[i], w_in[i], attn_sinks[i], rwkv_mu[i], rwkv_w0[i], rwkv_w_up[i],
                        rwkv_a0[i], rwkv_a_up[i], rwkv_g_up[i], rwkv_k_k[i], rwkv_k_a[i], rwkv_r_k[i],
                        rwkv_ln_w[i], rwkv_ln_b[i], lb_layers[i], hgrn_norm[i], w_branch[i], w_o[i])
        x = ffn_half_step(x, ffn2_norm[i], bf(ffn2_w_gate[i]), bf(ffn2_w_up[i]), bf(ffn2_w_down[i]))
        x = ple_step(x, p[i].reshape(n, -1), ple_norm[i], bf(ple_w_gate[i]), bf(ple_w_proj[i]), final_norm,
                     final=(i == depth - 1))
    return x.reshape(bsz, seq_len, d)
```

```python
import functools

import jax
import jax.numpy as jnp
from jax import lax
from jax.experimental import pallas as pl
from jax.experimental.pallas import tpu as pltpu

F32 = jnp.float32
BF16 = jnp.bfloat16

NORM_EPS = 1e-6
FFN_HALF = 0.5
MASK_VALUE = -1e30
HEAD_DIM = 64
ATTN_HEADS = 8
ATTN_BLOCK = 128
ATTN_SCALE = HEAD_DIM ** -0.5
BRANCH_WIDTH = 512
RWKV_GN_EPS = 64e-5
RWKV_DECAY_SCALE = 0.6065306597126334
LOG2_E = 1.4426950408889634
HGRN_HEADS = 4
HGRN_DIM = 128

LANES = 128
VMEM_LIMIT_BYTES = 56 * 1024 * 1024

FFN_ROWS = 512
ROW_TILE = 512
RWKV_CHUNK = 32
RWKV_STEP_CHUNKS = 4
RWKV_GROUP_HEADS = 4
RWKV_GROUP_LANES = RWKV_GROUP_HEADS * HEAD_DIM
HGRN_CHUNK = 16
HGRN_BLOCK = 128
HGRN_UNROLL = 8


def _params(semantics):
    return pltpu.CompilerParams(dimension_semantics=semantics, vmem_limit_bytes=VMEM_LIMIT_BYTES)


def _mm(a, b):
    return jnp.dot(a.astype(BF16), b.astype(BF16), preferred_element_type=F32)


def _mm_nt(a, b):
    return lax.dot_general(a.astype(BF16), b.astype(BF16), (((1,), (1,)), ((), ())), preferred_element_type=F32)


def _mm_tn(a, b):
    return lax.dot_general(a.astype(BF16), b.astype(BF16), (((0,), (0,)), ((), ())), preferred_element_type=F32)


def _split(x, pieces):
    out = []
    for _ in range(pieces - 1):
        p = x.astype(BF16)
        out.append(p)
        x = x - p.astype(F32)
    out.append(x.astype(BF16))
    return out


def _sel_mm(sel, x, pieces=3):
    sel = sel.astype(BF16)
    return sum(jnp.dot(sel, p, preferred_element_type=F32) for p in _split(x, pieces))


def _rms(x, g):
    ms = jnp.mean(x * x, axis=-1, keepdims=True)
    return x * lax.rsqrt(ms + NORM_EPS) * g


def _full(shape):
    return pl.BlockSpec(shape, lambda *_: (0,) * len(shape))


def _resident(shape):
    return pl.BlockSpec(shape, lambda *_: (0,) * len(shape), pipeline_mode=pl.Buffered(1))


def _ffn_body(x_ref, g_ref, wg_ref, wu_ref, wd_ref, *rest, with_ple, final):
    o_ref = rest[-1]
    x = x_ref[...]
    h = _rms(x, g_ref[...]).astype(BF16)
    gate = jnp.dot(h, wg_ref[...], preferred_element_type=F32)
    up = jnp.dot(h, wu_ref[...], preferred_element_type=F32)
    act = (gate * jax.nn.sigmoid(gate) * up).astype(BF16)
    x = x + FFN_HALF * jnp.dot(act, wd_ref[...], preferred_element_type=F32)
    if with_ple:
        p_ref, pg_ref, pwg_ref, pwp_ref, gf_ref = rest[:-1]
        pgate = jax.nn.sigmoid(_mm(_rms(x, pg_ref[...]), pwg_ref[...]))
        x = x + pgate * _mm(p_ref[...], pwp_ref[...])
        if final:
            x = _rms(x, gf_ref[...])
    o_ref[...] = x


def ffn_half_step(x, g, wg, wu, wd, ple=None, final=False):
    n, d = x.shape
    f = wg.shape[1]
    tm = min(FFN_ROWS, n)
    row = lambda t: t.reshape(1, d)
    args = [x, row(g), wg, wu, wd]
    in_specs = [pl.BlockSpec((tm, d), lambda i: (i, 0)), _full((1, d)),
                _resident((d, f)), _resident((d, f)), _resident((f, d))]
    if ple is not None:
        p, pg, pwg, pwp, gf = ple
        pw = p.shape[1]
        args += [p, row(pg), pwg, pwp, row(gf)]
        in_specs += [pl.BlockSpec((tm, pw), lambda i: (i, 0)), _full((1, d)), _resident((d, d)),
                     _resident((pw, d)), _full((1, d))]
    return pl.pallas_call(
        functools.partial(_ffn_body, with_ple=ple is not None, final=final),
        grid=(n // tm,),
        in_specs=in_specs,
        out_specs=pl.BlockSpec((tm, d), lambda i: (i, 0)),
        out_shape=jax.ShapeDtypeStruct((n, d), F32),
        compiler_params=_params(("parallel",)),
        name="ffn_half_step",
    )(*args)


def _proj_body(x_ref, g_ref, *refs):
    k = len(refs) // 2
    h = _rms(x_ref[...], g_ref[...]).astype(BF16)
    for w_ref, o_ref in zip(refs[:k], refs[k:]):
        o_ref[...] = jnp.dot(h, w_ref[...], preferred_element_type=F32).astype(o_ref.dtype)


def norm_project(x, g, weights, out_dtypes):
    n, d = x.shape
    tm = min(ROW_TILE, n)
    return pl.pallas_call(
        _proj_body,
        grid=(n // tm,),
        in_specs=[pl.BlockSpec((tm, d), lambda i: (i, 0)), _full((1, d))] + [_resident(w.shape) for w in weights],
        out_specs=[pl.BlockSpec((tm, w.shape[1]), lambda i: (i, 0)) for w in weights],
        out_shape=[jax.ShapeDtypeStruct((n, w.shape[1]), dt) for w, dt in zip(weights, out_dtypes)],
        compiler_params=_params(("parallel",)),
        name="norm_project",
    )(x, g.reshape(1, d), *weights)


def _attn_body(q_ref, kc_ref, kp_ref, vc_ref, vp_ref, sink_ref, o_ref, *, blocks_per_seq):
    blk = ATTN_BLOCK
    n = pl.program_id(0)
    first = (n % blocks_per_seq == 0).astype(jnp.int32)
    kb = jnp.concatenate([kp_ref[...], kc_ref[...]], axis=0).astype(F32)
    vb = jnp.concatenate([vp_ref[...], vc_ref[...]], axis=0).astype(F32)
    lane = lax.broadcasted_iota(jnp.int32, kb.shape, 1)
    low = lane < HEAD_DIM
    kb_sw = pltpu.roll(kb, HEAD_DIM, 1)
    vb_sw = pltpu.roll(vb, HEAD_DIM, 1)

    def placed(x, x_sw, kv_head, low_half):
        src = x if (kv_head == 0) == low_half else x_sw
        return jnp.where(low if low_half else jnp.logical_not(low), src, 0.0).astype(BF16)

    qi = lax.broadcasted_iota(jnp.int32, (blk, 2 * blk), 0)
    kj = lax.broadcasted_iota(jnp.int32, (blk, 2 * blk), 1)
    rel = blk + qi - kj
    mask = (rel >= 0) & (rel < blk) & (kj >= first * blk)

    probs = [(kv, half) for kv in range(2) for half in range(2)]
    qs = [jnp.concatenate([q_ref[:, (2 * kv) * LANES:(2 * kv + 1) * LANES],
                           q_ref[:, (2 * kv + 1) * LANES:(2 * kv + 2) * LANES]], axis=0) for kv in range(2)]
    k_ext = [placed(kb, kb_sw, kv, half == 0) for kv, half in probs]
    v_ext = [placed(vb, vb_sw, kv, half == 0) for kv, half in probs]
    scores = [_mm_nt(qs[kv], k_ext[i]) for i, (kv, _) in enumerate(probs)]
    weights = []
    for i, (kv, half) in enumerate(probs):
        parts = []
        for pair in range(2):
            head = 4 * kv + 2 * pair + half
            s = jnp.where(mask, scores[i][pair * blk:(pair + 1) * blk] * ATTN_SCALE, MASK_VALUE)
            sink = jnp.max(sink_ref[head:head + 1, :], axis=-1, keepdims=True)
            m = jnp.maximum(jnp.max(s, axis=-1, keepdims=True), sink)
            e = jnp.exp(s - m)
            denom = jnp.sum(e, axis=-1, keepdims=True) + jnp.exp(sink - m)
            parts.append((e * (1.0 / denom)).astype(BF16))
        weights.append(jnp.concatenate(parts, axis=0))
    outs = [_mm(weights[i], v_ext[i]) for i in range(len(probs))]
    for kv in range(2):
        out = outs[2 * kv] + outs[2 * kv + 1]
        o_ref[:, (2 * kv) * LANES:(2 * kv + 1) * LANES] = out[:blk].astype(o_ref.dtype)
        o_ref[:, (2 * kv + 1) * LANES:(2 * kv + 2) * LANES] = out[blk:].astype(o_ref.dtype)


def sliding_window_attention(z_attn, sinks, seq_len):
    n = z_attn.shape[0]
    blk = ATTN_BLOCK
    qw = ATTN_HEADS * HEAD_DIM
    kcol = qw // LANES
    vcol = kcol + 1
    sink_rows = jnp.broadcast_to(sinks.astype(F32)[:, None], (ATTN_HEADS, LANES))
    prev = lambda i: jnp.maximum(i - 1, 0)
    return pl.pallas_call(
        functools.partial(_attn_body, blocks_per_seq=seq_len // blk),
        grid=(n // blk,),
        in_specs=[
            pl.BlockSpec((blk, qw), lambda i: (i, 0)),
            pl.BlockSpec((blk, LANES), lambda i: (i, kcol)),
            pl.BlockSpec((blk, LANES), lambda i: (prev(i), kcol)),
            pl.BlockSpec((blk, LANES), lambda i: (i, vcol)),
            pl.BlockSpec((blk, LANES), lambda i: (prev(i), vcol)),
            _full((ATTN_HEADS, LANES)),
        ],
        out_specs=pl.BlockSpec((blk, qw), lambda i: (i, 0)),
        out_shape=jax.ShapeDtypeStruct((n, qw), BF16),
        compiler_params=_params(("parallel",)),
        name="sliding_window_attention",
    )(z_attn, z_attn, z_attn, z_attn, z_attn, sink_rows)


def _head_ones(width, head):
    idx = jnp.arange(width) // head
    return (idx[:, None] == idx[None, :]).astype(BF16)


def _chunk_selectors(rows, chunk):
    r = jnp.arange(rows)
    prefix = (r[:, None] // chunk == r[None, :] // chunk) & (r[None, :] <= r[:, None])
    member = jnp.arange(rows // chunk)[:, None] == r[None, :] // chunk
    return prefix.astype(BF16), member.astype(BF16)


def _rwkv_pre_body(z_ref, zp_ref, mu_ref, w0_ref, wup_ref, a0_ref, aup_ref, gup_ref, kk_ref, ka_ref, rk_ref,
                   ones_ref, prefix_ref, member_ref,
                   ad_o, rd_o, bd_o, kd_o, be_o, ke_o, v_o, gam_o, bonus_o, g_o, *, tiles_per_seq):
    bw = BRANCH_WIDTH
    i = pl.program_id(0)
    z = z_ref[...]
    keep = (i % tiles_per_seq != 0).astype(F32)
    prev_row = zp_ref[7:8, :] * keep
    row = lax.broadcasted_iota(jnp.int32, z.shape, 0)
    z_prev = jnp.where(row == 0, prev_row, pltpu.roll(z, 1, 0))
    z = z + mu_ref[...] * (z_prev - z)
    r = z[:, 0:bw]
    k = z[:, bw:2 * bw]
    v = z[:, 2 * bw:3 * bw]
    x_wa = z[:, 3 * bw:3 * bw + LANES]
    x_g = z[:, 3 * bw + LANES:3 * bw + 2 * LANES]
    d = w0_ref[...] + _mm(jnp.tanh(x_wa), wup_ref[...])
    lw = -RWKV_DECAY_SCALE * jax.nn.sigmoid(d)
    a_gate = jax.nn.sigmoid(a0_ref[...] + _mm(x_wa, aup_ref[...]))
    g = _mm(jax.nn.sigmoid(x_g), gup_ref[...])
    kk = k * kk_ref[...]
    ones = ones_ref[...]
    kk = kk * lax.rsqrt(jnp.maximum(_mm(kk * kk, ones), 1e-24))
    k = k * (1.0 + (a_gate - 1.0) * ka_ref[...])
    chunk = RWKV_CHUNK
    tm = z.shape[0]
    cs = _sel_mm(prefix_ref[...], lw, pieces=2)
    tot_c = _sel_mm(member_ref[...], lw)
    gam_o[:, 0, :] = jnp.exp(tot_c)
    tot = jnp.broadcast_to(tot_c[:, None, :], (tm // chunk, chunk, bw)).reshape(tm, bw)
    dec_out = jnp.exp(-cs)
    dec_end = jnp.exp(tot - cs)
    b = kk * a_gate
    ad_o[...] = (-kk * jnp.exp(cs - lw)).astype(ad_o.dtype)
    rd_o[...] = (r * jnp.exp(cs)).astype(rd_o.dtype)
    bd_o[...] = (b * dec_out).astype(bd_o.dtype)
    kd_o[...] = (k * dec_out).astype(kd_o.dtype)
    be_o[...] = (b * dec_end).astype(be_o.dtype)
    ke_o[...] = (k * dec_end).astype(ke_o.dtype)
    v_o[...] = v.astype(v_o.dtype)
    bonus_o[...] = _mm(r * k * rk_ref[...], ones) * v
    g_o[...] = g


def rwkv_prepare(z, mu, w0, w_up, a0, a_up, g_up, k_k, k_a, r_k, seq_len):
    n, zw = z.shape
    bw = BRANCH_WIDTH
    tm = min(ROW_TILE, seq_len)
    half = w_up.shape[0]
    zeros = jnp.zeros((LANES - half, bw), BF16)
    wup_ext = jnp.concatenate([w_up.astype(BF16), zeros], axis=0)
    aup_ext = jnp.concatenate([zeros, a_up.astype(BF16)], axis=0)
    row = lambda t: t.reshape(1, -1).astype(F32)
    tile = pl.BlockSpec((tm, bw), lambda i: (i, 0))
    cpt = tm // RWKV_CHUNK
    return pl.pallas_call(
        functools.partial(_rwkv_pre_body, tiles_per_seq=seq_len // tm),
        grid=(n // tm,),
        in_specs=[
            pl.BlockSpec((tm, zw), lambda i: (i, 0)),
            pl.BlockSpec((8, zw), lambda i: (jnp.maximum(i * (tm // 8) - 1, 0), 0)),
            _full((1, zw)), _full((1, bw)), _full((LANES, bw)), _full((1, bw)), _full((LANES, bw)),
            _full((LANES, bw)), _full((1, bw)), _full((1, bw)), _full((1, bw)),
            _resident((bw, bw)), _resident((tm, tm)), _resident((cpt, tm)),
        ],
        out_specs=[tile] * 7 + [pl.BlockSpec((cpt, 1, bw), lambda i: (i, 0, 0)), tile, tile],
        out_shape=[jax.ShapeDtypeStruct((n, bw), BF16)] * 7
        + [jax.ShapeDtypeStruct((n // RWKV_CHUNK, 1, bw), F32)] + [jax.ShapeDtypeStruct((n, bw), F32)] * 2,
        compiler_params=_params(("parallel",)),
        name="rwkv_prepare",
    )(z, z, row(mu), row(w0), wup_ext, row(a0), aup_ext, g_up.astype(BF16), row(k_k), row(k_a), row(r_k),
      _head_ones(bw, HEAD_DIM), *_chunk_selectors(tm, RWKV_CHUNK))


def _rwkv_scan_body(ad_ref, rd_ref, bd_ref, kd_ref, be_ref, ke_ref, v_ref, gam_ref, y_ref, s_ref):
    c = pl.program_id(0)

    @pl.when(c == 0)
    def _():
        s_ref[...] = jnp.zeros_like(s_ref)

    batch = v_ref.shape[0]
    chunk = RWKV_CHUNK
    n_chunks = v_ref.shape[1] // chunk
    gh, gl = RWKV_GROUP_HEADS, RWKV_GROUP_LANES
    rows = gh * chunk
    row_head = lax.broadcasted_iota(jnp.int32, (rows, gl), 0) // chunk
    lane_head = lax.broadcasted_iota(jnp.int32, (rows, gl), 1) // HEAD_DIM
    own = row_head == lane_head
    ii = lax.broadcasted_iota(jnp.int32, (rows, rows), 0)
    jj = lax.broadcasted_iota(jnp.int32, (rows, rows), 1)
    strict = jj < ii
    incl = jj <= ii
    eye = (ii == jj).astype(F32)

    def stack(x):
        return jnp.where(own, jnp.concatenate([x] * gh, axis=0), 0.0).astype(BF16)

    chains = [(bi, g) for bi in range(batch) for g in range(BRANCH_WIDTH // gl)]
    probs = [(c, bi, g) for c in range(n_chunks) for bi, g in chains]
    npb = range(len(probs))
    operand_refs = (ad_ref, rd_ref, bd_ref, kd_ref, be_ref, ke_ref, v_ref)
    ar, bk, v_s, be_ke, tot = [], [], [], [], []
    for c, bi, g in probs:
        rs = slice(c * chunk, (c + 1) * chunk)
        sl = slice(g * gl, (g + 1) * gl)
        ad, rd, bd, kd, be, ke, v = (ref[bi, rs, sl] for ref in operand_refs)
        ar.append(jnp.concatenate([stack(ad), stack(rd)], axis=0))
        bk.append(jnp.concatenate([stack(bd), stack(kd)], axis=0))
        v_s.append(stack(v))
        be_ke.append(jnp.concatenate([stack(be), stack(ke)], axis=0))
        tot.append(gam_ref[bi, c, :, sl])
    gram = [_mm_nt(ar[i], bk[i]) for i in npb]
    nmat = [jnp.where(strict, gram[i][:rows, :rows], 0.0).astype(BF16) for i in npb]
    m_ak = [jnp.where(strict, gram[i][:rows, rows:], 0.0).astype(BF16) for i in npb]
    m_rbk = [jnp.where(jnp.concatenate([incl, incl], axis=1), gram[i][rows:, :], 0.0).astype(BF16) for i in npb]
    tinv = [eye + nmat[i].astype(F32) for i in npb]
    for _ in range(chunk.bit_length() - 2):
        nmat = [_mm(nmat[i], nmat[i]).astype(BF16) for i in npb]
        tinv = [tinv[i] + _mm(tinv[i], nmat[i]) for i in npb]
    tinv = [tinv[i].astype(BF16) for i in npb]
    x0 = [_mm(m_ak[i], v_s[i]) for i in npb]

    sts = [s_ref[bi, g] for bi, g in chains]
    ncn = len(chains)
    for c in range(n_chunks):
        ids = [c * ncn + j for j in range(ncn)]
        ars = [_mm_nt(ar[i], sts[j]) for j, i in enumerate(ids)]
        sa = [_mm(tinv[i], ars[j][:rows] + x0[i]).astype(BF16) for j, i in enumerate(ids)]
        sv = [jnp.concatenate([sa[j], v_s[i]], axis=0) for j, i in enumerate(ids)]
        y_s = [ars[j][rows:] + _mm(m_rbk[i], sv[j]) for j, i in enumerate(ids)]
        upd = [_mm_tn(sv[j], be_ke[i]) for j, i in enumerate(ids)]
        sts = [sts[j] * tot[i] + upd[j] for j, i in enumerate(ids)]
        for j, (bi, g) in enumerate(chains):
            y = y_s[j][0:chunk]
            for h in range(1, gh):
                y = y + y_s[j][h * chunk:(h + 1) * chunk]
            y_ref[bi, c * chunk:(c + 1) * chunk, g * gl:(g + 1) * gl] = y
    for j, (bi, g) in enumerate(chains):
        s_ref[bi, g] = sts[j]


def rwkv_scan(ad, rd, bd, kd, be, ke, v, gam):
    bsz, seq, bw = v.shape
    steps = min(RWKV_STEP_CHUNKS, seq // RWKV_CHUNK)
    span = steps * RWKV_CHUNK
    blk = pl.BlockSpec((bsz, span, bw), lambda c: (0, c, 0))
    groups = bw // RWKV_GROUP_LANES
    return pl.pallas_call(
        _rwkv_scan_body,
        grid=(seq // span,),
        in_specs=[blk] * 7 + [pl.BlockSpec((bsz, steps, 1, bw), lambda c: (0, c, 0, 0))],
        out_specs=blk,
        out_shape=jax.ShapeDtypeStruct((bsz, seq, bw), F32),
        scratch_shapes=[pltpu.VMEM((bsz, groups, RWKV_GROUP_LANES, RWKV_GROUP_LANES), F32)],
        compiler_params=_params(("arbitrary",)),
        name="rwkv_scan",
    )(ad, rd, bd, kd, be, ke, v, gam)


def _rwkv_post_body(y_ref, bonus_ref, g_ref, lnw_ref, lnb_ref, ones_ref, o_ref):
    ones = ones_ref[...]
    y = y_ref[...]
    mean = _mm(y, ones) * (1.0 / HEAD_DIM)
    yc = y - mean
    var = _mm(yc * yc, ones) * (1.0 / HEAD_DIM)
    yn = yc * lax.rsqrt(var + RWKV_GN_EPS) * lnw_ref[...] + lnb_ref[...]
    o_ref[...] = ((yn + bonus_ref[...]) * g_ref[...]).astype(o_ref.dtype)


def rwkv_finish(y, bonus, g, ln_w, ln_b):
    n, bw = y.shape
    tm = min(ROW_TILE, n)
    tile = pl.BlockSpec((tm, bw), lambda i: (i, 0))
    row = lambda t: t.reshape(1, bw).astype(F32)
    return pl.pallas_call(
        _rwkv_post_body,
        grid=(n // tm,),
        in_specs=[tile] * 3 + [_full((1, bw))] * 2 + [_resident((bw, bw))],
        out_specs=tile,
        out_shape=jax.ShapeDtypeStruct((n, bw), BF16),
        compiler_params=_params(("parallel",)),
        name="rwkv_finish",
    )(y, bonus, g, row(ln_w), row(ln_b), _head_ones(bw, HEAD_DIM))


def _hgrn_body(z_ref, lb_ref, ng_ref, o_ref, s_ref, c_ref):
    c = pl.program_id(0)

    @pl.when(c == 0)
    def _():
        s_ref[...] = jnp.zeros_like(s_ref)

    batch, rows, _ = z_ref.shape
    bw = BRANCH_WIDTH
    hd = HGRN_DIM
    ch = HGRN_CHUNK
    half = ch // 2
    upto = (lax.broadcasted_iota(jnp.int32, (ch, ch), 1) <= lax.broadcasted_iota(jnp.int32, (ch, ch), 0))
    row8 = lax.broadcasted_iota(jnp.int32, (half, hd), 0)
    lane8 = lax.broadcasted_iota(jnp.int32, (half, hd), 1)
    chains = [(bi, h) for bi in range(batch) for h in range(HGRN_HEADS)]

    def gates(rs, slot):
        lb = lb_ref[...]
        q, qe, ke, gam, og, b2, val = [], [], [], [], [], [], []
        for bi in range(batch):
            zq = z_ref[bi, rs, 0:bw]
            zf = z_ref[bi, rs, bw:2 * bw]
            zo = z_ref[bi, rs, 3 * bw:4 * bw]
            forget = lb + (1.0 - lb) * jax.nn.sigmoid(zf)
            key = 1.0 - forget
            b = _sel_mm(upto, jnp.log(forget))
            tot = b[ch - 1:ch, :]
            qb = zq * jax.nn.sigmoid(zq)
            q.append(qb)
            qe.append((qb * jnp.exp(b)).astype(BF16))
            ke.append((key * jnp.exp(tot - b)).astype(BF16))
            gam.append(jnp.exp(tot))
            og.append(zo * jax.nn.sigmoid(zo))
            val.append(z_ref[bi, rs, 2 * bw:3 * bw].astype(BF16))
            b2.append(b * LOG2_E)
            c_ref[slot, bi] = (b - jnp.log(key)) * LOG2_E
        return q, qe, ke, gam, og, b2, val

    def chunk_weights(q, b2, slot):
        attn = []
        for bi, h in chains:
            hs = slice(h * hd, (h + 1) * hd)
            b_lo = b2[bi][0:half, hs]
            b_hi = b2[bi][half:ch, hs]
            q_lo = q[bi][0:half, hs]
            q_hi = q[bi][half:ch, hs]
            a_lo = jnp.zeros((half, hd), F32)
            a_hi = jnp.zeros((half, hd), F32)
            for s in range(ch):
                c_s = c_ref[slot, bi, s:s + 1, hs]
                if s < half:
                    w = q_lo * jnp.exp2(jnp.where(row8 >= s, b_lo - c_s, MASK_VALUE))
                    a_lo = jnp.where(lane8 == s, jnp.sum(w, axis=-1, keepdims=True), a_lo)
                    w = q_hi * jnp.exp2(b_hi - c_s)
                else:
                    w = q_hi * jnp.exp2(jnp.where(row8 >= s - half, b_hi - c_s, MASK_VALUE))
                a_hi = jnp.where(lane8 == s, jnp.sum(w, axis=-1, keepdims=True), a_hi)
            attn.append(jnp.concatenate([a_lo, a_hi], axis=0)[:, 0:ch].astype(BF16))
        return attn

    def chunk_group(jj, carry):
        slots = range(HGRN_UNROLL)
        rss = [pl.ds(pl.multiple_of((jj * HGRN_UNROLL + u) * ch, ch), ch) for u in slots]
        gated = [gates(rss[u], u) for u in slots]
        attn = [chunk_weights(gated[u][0], gated[u][5], u) for u in slots]
        sts = [s_ref[bi, h] for bi, h in chains]
        outs = []
        for u in slots:
            _, qe, ke, gam, og, _, val = gated[u]
            hsl = [slice(h * hd, (h + 1) * hd) for _, h in chains]
            inter = [_mm_nt(qe[bi][:, hsl[i]], sts[i]) for i, (bi, _) in enumerate(chains)]
            upd = [_mm_tn(val[bi][:, hsl[i]], ke[bi][:, hsl[i]]) for i, (bi, _) in enumerate(chains)]
            intra = [_mm(attn[u][i], val[bi][:, hsl[i]]) for i, (bi, _) in enumerate(chains)]
            outs.append([_rms(inter[i] + intra[i], ng_ref[...]) * og[bi][:, hsl[i]]
                         for i, (bi, _) in enumerate(chains)])
            sts = [sts[i] * gam[bi][:, hsl[i]] + upd[i] for i, (bi, _) in enumerate(chains)]
        for u in slots:
            for i, (bi, h) in enumerate(chains):
                o_ref[bi, rss[u], h * hd:(h + 1) * hd] = outs[u][i].astype(o_ref.dtype)
        for i, (bi, h) in enumerate(chains):
            s_ref[bi, h] = sts[i]
        return carry

    lax.fori_loop(0, rows // (ch * HGRN_UNROLL), chunk_group, 0)


def hgrn_mix(z, lb, norm_g):
    bsz, seq, zw = z.shape
    bw = BRANCH_WIDTH
    rows = min(HGRN_BLOCK, seq)
    blk = pltpu.VMEM((HGRN_UNROLL, bsz, HGRN_CHUNK, bw), F32)
    return pl.pallas_call(
        _hgrn_body,
        grid=(seq // rows,),
        in_specs=[pl.BlockSpec((bsz, rows, zw), lambda c: (0, c, 0)), _full((1, bw)), _full((1, HGRN_DIM))],
        out_specs=pl.BlockSpec((bsz, rows, bw), lambda c: (0, c, 0)),
        out_shape=jax.ShapeDtypeStruct((bsz, seq, bw), BF16),
        scratch_shapes=[pltpu.VMEM((bsz, HGRN_HEADS, HGRN_DIM, HGRN_DIM), F32), blk],
        compiler_params=_params(("arbitrary",)),
        name="hgrn_mix",
    )(z, lb.reshape(1, bw).astype(F32), norm_g.reshape(1, HGRN_DIM).astype(F32))


def _merge_body(x_ref, g_ref, ya_ref, yb_ref, yc_ref, wgate_ref, wb_ref, wo_ref, o_ref):
    x = x_ref[...]
    d = x.shape[1]
    h = _rms(x, g_ref[...]).astype(BF16)
    merged = jnp.zeros(x.shape, F32)
    for n, y_ref in enumerate((ya_ref, yb_ref, yc_ref)):
        gate = jax.nn.sigmoid(jnp.dot(h, wgate_ref[:, n * d:(n + 1) * d], preferred_element_type=F32))
        merged = merged + gate * jnp.dot(y_ref[...], wb_ref[n], preferred_element_type=F32)
    o_ref[...] = x + _mm(merged, wo_ref[...])


def merge_branches(x, g, ya, yb, yc, w_gate, w_branch, w_o):
    n, d = x.shape
    bw = ya.shape[1]
    tm = min(ROW_TILE, n)
    ytile = pl.BlockSpec((tm, bw), lambda i: (i, 0))
    return pl.pallas_call(
        _merge_body,
        grid=(n // tm,),
        in_specs=[pl.BlockSpec((tm, d), lambda i: (i, 0)), _full((1, d)), ytile, ytile, ytile,
                  _resident(w_gate.shape), _resident(w_branch.shape), _resident(w_o.shape)],
        out_specs=pl.BlockSpec((tm, d), lambda i: (i, 0)),
        out_shape=jax.ShapeDtypeStruct((n, d), F32),
        compiler_params=_params(("parallel",)),
        name="merge_branches",
    )(x, g.reshape(1, d), ya, yb, yc, w_gate, w_branch, w_o)


def token_mixer(x, seq_len, mix_norm, w_in, attn_sinks, rwkv_mu, rwkv_w0, rwkv_w_up, rwkv_a0, rwkv_a_up, rwkv_g_up,
                rwkv_k_k, rwkv_k_a, rwkv_r_k, rwkv_ln_w, rwkv_ln_b, lb, hgrn_norm, w_branch, w_o):
    n, d = x.shape
    bsz = n // seq_len
    bw = BRANCH_WIDTH
    attn_w = (ATTN_HEADS + 2 * (ATTN_HEADS // 4)) * HEAD_DIM
    rwkv_w = rwkv_mu.shape[0]
    hgrn_w = 4 * bw
    o1, o2, o3 = attn_w, attn_w + rwkv_w, attn_w + rwkv_w + hgrn_w
    w_attn, w_rwkv, w_hgrn, w_gate = (w_in[:, a:b].astype(BF16) for a, b in ((0, o1), (o1, o2), (o2, o3), (o3, None)))
    z_attn, z_rwkv, z_hgrn = norm_project(x, mix_norm, (w_attn, w_rwkv, w_hgrn), (BF16, F32, F32))

    y_a = sliding_window_attention(z_attn, attn_sinks, seq_len)

    *operands, gam, bonus, g = rwkv_prepare(z_rwkv, rwkv_mu, rwkv_w0, rwkv_w_up, rwkv_a0, rwkv_a_up, rwkv_g_up,
                                            rwkv_k_k, rwkv_k_a, rwkv_r_k, seq_len)
    y = rwkv_scan(*(t.reshape(bsz, seq_len, bw) for t in operands),
                  gam.reshape(bsz, seq_len // RWKV_CHUNK, 1, bw)).reshape(n, bw)
    y_b = rwkv_finish(y, bonus, g, rwkv_ln_w, rwkv_ln_b)

    y_c = hgrn_mix(z_hgrn.reshape(bsz, seq_len, hgrn_w), lb, hgrn_norm).reshape(n, bw)

    return merge_branches(x, mix_norm, y_a, y_b, y_c, w_gate, w_branch.astype(BF16), w_o.astype(BF16))


def kernel(x, p, ffn1_norm, ffn1_w_gate, ffn1_w_up, ffn1_w_down, mix_norm, w_in, attn_sinks, rwkv_mu, rwkv_w0,
           rwkv_w_up, rwkv_a0, rwkv_a_up, rwkv_g_up, rwkv_k_k, rwkv_k_a, rwkv_r_k, rwkv_ln_w, rwkv_ln_b, hgrn_lb,
           hgrn_norm, w_branch, w_o, ffn2_norm, ffn2_w_gate, ffn2_w_up, ffn2_w_down, ple_norm, ple_w_gate,
           ple_w_proj, final_norm):
    bsz, seq_len, d = x.shape
    depth = p.shape[0]
    n = bsz * seq_len
    lb_all = jax.nn.softmax(hgrn_lb.astype(F32), axis=0)
    lb_layers = jnp.clip(jnp.cumsum(lb_all, axis=0) - lb_all[0:1], 0.0, 1.0 - 1e-6)
    bf = lambda t: t.astype(BF16)
    x = x.reshape(n, d)
    for i in range(depth):
        x = ffn_half_step(x, ffn1_norm[i], bf(ffn1_w_gate[i]), bf(ffn1_w_up[i]), bf(ffn1_w_down[i]))
        x = token_mixer(x, seq_len, mix_norm[i], w_in[i], attn_sinks[i], rwkv_mu[i], rwkv_w0[i], rwkv_w_up[i],
                        rwkv_a0[i], rwkv_a_up[i], rwkv_g_up[i], rwkv_k_k[i], rwkv_k_a[i], rwkv_r_k[i],
                        rwkv_ln_w[i], rwkv_ln_b[i], lb_layers[i], hgrn_norm[i], w_branch[i], w_o[i])
        ple = (p[i].reshape(n, -1), ple_norm[i], bf(ple_w_gate[i]), bf(ple_w_proj[i]), final_norm)
        x = ffn_half_step(x, ffn2_norm[i], bf(ffn2_w_gate[i]), bf(ffn2_w_up[i]), bf(ffn2_w_down[i]), ple=ple,
                          final=(i == depth - 1))
    return x.reshape(bsz, seq_len, d)
```

```python
import functools

import jax
import jax.numpy as jnp
from jax import lax
from jax.experimental import pallas as pl
from jax.experimental.pallas import tpu as pltpu

F32 = jnp.float32
BF16 = jnp.bfloat16

NORM_EPS = 1e-6
FFN_HALF = 0.5
MASK_VALUE = -1e30
HEAD_DIM = 64
ATTN_HEADS = 8
ATTN_BLOCK = 128
ATTN_SCALE = HEAD_DIM ** -0.5
assert ATTN_SCALE == 0.125
BRANCH_WIDTH = 512
RWKV_GN_EPS = 64e-5
RWKV_DECAY_SCALE = 0.6065306597126334
LOG2_E = 1.4426950408889634
HGRN_HEADS = 4
HGRN_DIM = 128

LANES = 128
VMEM_LIMIT_BYTES = 56 * 1024 * 1024

FFN_ROWS = 512
ROW_TILE = 512
ATTN_STEP_BLOCKS = 4
RWKV_CHUNK = 32
RWKV_STEP_CHUNKS = 4
RWKV_GROUP_HEADS = 4
RWKV_GROUP_LANES = RWKV_GROUP_HEADS * HEAD_DIM
HGRN_CHUNK = 16
HGRN_BLOCK = 256
HGRN_UNROLL = 8


def _params(semantics):
    return pltpu.CompilerParams(dimension_semantics=semantics, vmem_limit_bytes=VMEM_LIMIT_BYTES)


def _mm(a, b):
    return jnp.dot(a.astype(BF16), b.astype(BF16), preferred_element_type=F32)


def _mm_nt(a, b):
    return lax.dot_general(a.astype(BF16), b.astype(BF16), (((1,), (1,)), ((), ())), preferred_element_type=F32)


def _mm_tn(a, b):
    return lax.dot_general(a.astype(BF16), b.astype(BF16), (((0,), (0,)), ((), ())), preferred_element_type=F32)


def _split(x, pieces):
    out = []
    for _ in range(pieces - 1):
        p = x.astype(BF16)
        out.append(p)
        x = x - p.astype(F32)
    out.append(x.astype(BF16))
    return out


def _sel_mm(sel, x, pieces=3):
    sel = sel.astype(BF16)
    return sum(jnp.dot(sel, p, preferred_element_type=F32) for p in _split(x, pieces))


def _rms(x, g):
    ms = jnp.mean(x * x, axis=-1, keepdims=True)
    return x * lax.rsqrt(ms + NORM_EPS) * g


def _full(shape):
    return pl.BlockSpec(shape, lambda *_: (0,) * len(shape))


def _resident(shape):
    return pl.BlockSpec(shape, lambda *_: (0,) * len(shape), pipeline_mode=pl.Buffered(1))


def _ffn_body(x_ref, g_ref, wg_ref, wu_ref, wd_ref, *rest, with_ple, final):
    o_ref = rest[-1]
    x = x_ref[...]
    h = _rms(x, g_ref[...]).astype(BF16)
    gate = jnp.dot(h, wg_ref[...], preferred_element_type=F32)
    up = jnp.dot(h, wu_ref[...], preferred_element_type=F32)
    act = (gate * jax.nn.sigmoid(gate) * up).astype(BF16)
    x = x + FFN_HALF * jnp.dot(act, wd_ref[...], preferred_element_type=F32)
    if with_ple:
        p_ref, pg_ref, pwg_ref, pwp_ref, gf_ref = rest[:-1]
        pgate = jax.nn.sigmoid(_mm(_rms(x, pg_ref[...]), pwg_ref[...]))
        x = x + pgate * _mm(p_ref[...], pwp_ref[...])
        if final:
            x = _rms(x, gf_ref[...])
    o_ref[...] = x


def ffn_half_step(x, g, wg, wu, wd, ple=None, final=False):
    n, d = x.shape
    f = wg.shape[1]
    tm = min(FFN_ROWS, n)
    row = lambda t: t.reshape(1, d)
    args = [x, row(g), wg, wu, wd]
    in_specs = [pl.BlockSpec((tm, d), lambda i: (i, 0)), _full((1, d)),
                _resident((d, f)), _resident((d, f)), _resident((f, d))]
    if ple is not None:
        p, pg, pwg, pwp, gf = ple
        pw = p.shape[1]
        args += [p, row(pg), pwg, pwp, row(gf)]
        in_specs += [pl.BlockSpec((tm, pw), lambda i: (i, 0)), _full((1, d)), _resident((d, d)),
                     _resident((pw, d)), _full((1, d))]
    return pl.pallas_call(
        functools.partial(_ffn_body, with_ple=ple is not None, final=final),
        grid=(n // tm,),
        in_specs=in_specs,
        out_specs=pl.BlockSpec((tm, d), lambda i: (i, 0)),
        out_shape=jax.ShapeDtypeStruct((n, d), F32),
        compiler_params=_params(("parallel",)),
        name="ffn_half_step",
    )(*args)


def _proj_body(x_ref, g_ref, *refs):
    k = len(refs) // 2
    h = _rms(x_ref[...], g_ref[...]).astype(BF16)
    for w_ref, o_ref in zip(refs[:k], refs[k:]):
        o_ref[...] = jnp.dot(h, w_ref[...], preferred_element_type=F32).astype(o_ref.dtype)


def norm_project(x, g, weights, out_dtypes):
    n, d = x.shape
    tm = min(ROW_TILE, n)
    return pl.pallas_call(
        _proj_body,
        grid=(n // tm,),
        in_specs=[pl.BlockSpec((tm, d), lambda i: (i, 0)), _full((1, d))] + [_resident(w.shape) for w in weights],
        out_specs=[pl.BlockSpec((tm, w.shape[1]), lambda i: (i, 0)) for w in weights],
        out_shape=[jax.ShapeDtypeStruct((n, w.shape[1]), dt) for w, dt in zip(weights, out_dtypes)],
        compiler_params=_params(("parallel",)),
        name="norm_project",
    )(x, g.reshape(1, d), *weights)


def _attn_body(q_ref, kc_ref, kp_ref, vc_ref, vp_ref, sink_ref, o_ref, *, blocks_per_seq):
    blk = ATTN_BLOCK
    n_sub = q_ref.shape[0] // blk
    step = pl.program_id(0)
    lane = lax.broadcasted_iota(jnp.int32, (2 * blk, LANES), 1)
    low = lane < HEAD_DIM
    qi = lax.broadcasted_iota(jnp.int32, (blk, 2 * blk), 0)
    kj = lax.broadcasted_iota(jnp.int32, (blk, 2 * blk), 1)
    rel = blk + qi - kj
    band = (rel >= 0) & (rel < blk)

    def placed(x, x_sw, kv_head, low_half):
        src = x if (kv_head == 0) == low_half else x_sw
        return jnp.where(low if low_half else jnp.logical_not(low), src, 0.0).astype(BF16)

    probs = [(j, kv, half) for j in range(n_sub) for kv in range(2) for half in range(2)]
    masks, k_ext, v_ext, qs = [], {}, {}, {}
    for j in range(n_sub):
        rows = slice(j * blk, (j + 1) * blk)
        prev = slice((j - 1) * blk, j * blk)
        kb = jnp.concatenate([kp_ref[...] if j == 0 else kc_ref[prev, :], kc_ref[rows, :]], axis=0).astype(F32)
        vb = jnp.concatenate([vp_ref[...] if j == 0 else vc_ref[prev, :], vc_ref[rows, :]], axis=0).astype(F32)
        kb_sw = pltpu.roll(kb, HEAD_DIM, 1)
        vb_sw = pltpu.roll(vb, HEAD_DIM, 1)
        first = ((step * n_sub + j) % blocks_per_seq == 0).astype(jnp.int32)
        masks.append(band & (kj >= first * blk))
        for kv in range(2):
            qs[j, kv] = ATTN_SCALE * jnp.concatenate([q_ref[rows, (2 * kv) * LANES:(2 * kv + 1) * LANES],
                                                      q_ref[rows, (2 * kv + 1) * LANES:(2 * kv + 2) * LANES]], axis=0)
            for half in range(2):
                k_ext[j, kv, half] = placed(kb, kb_sw, kv, half == 0)
                v_ext[j, kv, half] = placed(vb, vb_sw, kv, half == 0)
    scores = [_mm_nt(qs[j, kv], k_ext[j, kv, half]) for j, kv, half in probs]
    sinks = [jnp.max(sink_ref[h:h + 1, :], axis=-1, keepdims=True) for h in range(ATTN_HEADS)]
    weights = []
    for i, (j, kv, half) in enumerate(probs):
        parts = []
        for pair in range(2):
            sink = sinks[4 * kv + 2 * pair + half]
            s = jnp.where(masks[j], scores[i][pair * blk:(pair + 1) * blk], MASK_VALUE)
            m = jnp.maximum(jnp.max(s, axis=-1, keepdims=True), sink)
            e = jnp.exp(s - m)
            denom = jnp.sum(e, axis=-1, keepdims=True) + jnp.exp(sink - m)
            parts.append((e * (1.0 / denom)).astype(BF16))
        weights.append(jnp.concatenate(parts, axis=0))
    outs = [_mm(weights[i], v_ext[p]) for i, p in enumerate(probs)]
    for j in range(n_sub):
        for kv in range(2):
            i = (j * 2 + kv) * 2
            out = outs[i] + outs[i + 1]
            rows = slice(j * blk, (j + 1) * blk)
            o_ref[rows, (2 * kv) * LANES:(2 * kv + 1) * LANES] = out[:blk].astype(o_ref.dtype)
            o_ref[rows, (2 * kv + 1) * LANES:(2 * kv + 2) * LANES] = out[blk:].astype(o_ref.dtype)


def sliding_window_attention(z_attn, sinks, seq_len):
    n = z_attn.shape[0]
    blk = ATTN_BLOCK
    span = min(ATTN_STEP_BLOCKS * blk, seq_len)
    sub = span // blk
    qw = ATTN_HEADS * HEAD_DIM
    kcol = qw // LANES
    vcol = kcol + 1
    sink_rows = jnp.broadcast_to(sinks.astype(F32)[:, None], (ATTN_HEADS, LANES))
    prev = lambda i: jnp.maximum(i * sub - 1, 0)
    return pl.pallas_call(
        functools.partial(_attn_body, blocks_per_seq=seq_len // blk),
        grid=(n // span,),
        in_specs=[
            pl.BlockSpec((span, qw), lambda i: (i, 0)),
            pl.BlockSpec((span, LANES), lambda i: (i, kcol)),
            pl.BlockSpec((blk, LANES), lambda i: (prev(i), kcol)),
            pl.BlockSpec((span, LANES), lambda i: (i, vcol)),
            pl.BlockSpec((blk, LANES), lambda i: (prev(i), vcol)),
            _full((ATTN_HEADS, LANES)),
        ],
        out_specs=pl.BlockSpec((span, qw), lambda i: (i, 0)),
        out_shape=jax.ShapeDtypeStruct((n, qw), BF16),
        compiler_params=_params(("parallel",)),
        name="sliding_window_attention",
    )(z_attn, z_attn, z_attn, z_attn, z_attn, sink_rows)


def _head_ones(width, head):
    idx = jnp.arange(width) // head
    return (idx[:, None] == idx[None, :]).astype(BF16)


def _chunk_selectors(rows, chunk):
    r = jnp.arange(rows)
    prefix = (r[:, None] // chunk == r[None, :] // chunk) & (r[None, :] <= r[:, None])
    member = jnp.arange(rows // chunk)[:, None] == r[None, :] // chunk
    return prefix.astype(BF16), member.astype(BF16)


def _rwkv_pre_body(z_ref, zp_ref, mu_ref, w0_ref, wup_ref, a0_ref, aup_ref, gup_ref, kk_ref, ka_ref, rk_ref,
                   ones_ref, prefix_ref, member_ref,
                   ad_o, rd_o, bd_o, kd_o, be_o, ke_o, v_o, gam_o, bonus_o, g_o, *, tiles_per_seq):
    bw = BRANCH_WIDTH
    i = pl.program_id(0)
    z = z_ref[...]
    keep = (i % tiles_per_seq != 0).astype(F32)
    prev_row = zp_ref[7:8, :] * keep
    row = lax.broadcasted_iota(jnp.int32, z.shape, 0)
    z_prev = jnp.where(row == 0, prev_row, pltpu.roll(z, 1, 0))
    z = z + mu_ref[...] * (z_prev - z)
    r = z[:, 0:bw]
    k = z[:, bw:2 * bw]
    v = z[:, 2 * bw:3 * bw]
    x_wa = z[:, 3 * bw:3 * bw + LANES]
    x_g = z[:, 3 * bw + LANES:3 * bw + 2 * LANES]
    d = w0_ref[...] + _mm(jnp.tanh(x_wa), wup_ref[...])
    lw = -RWKV_DECAY_SCALE * jax.nn.sigmoid(d)
    a_gate = jax.nn.sigmoid(a0_ref[...] + _mm(x_wa, aup_ref[...]))
    g = _mm(jax.nn.sigmoid(x_g), gup_ref[...])
    kk = k * kk_ref[...]
    ones = ones_ref[...]
    kk = kk * lax.rsqrt(jnp.maximum(_mm(kk * kk, ones), 1e-24))
    k = k * (1.0 + (a_gate - 1.0) * ka_ref[...])
    chunk = RWKV_CHUNK
    tm = z.shape[0]
    cs = _sel_mm(prefix_ref[...], lw, pieces=2)
    tot_c = _sel_mm(member_ref[...], lw)
    gam_o[:, 0, :] = jnp.exp(tot_c)
    tot = jnp.broadcast_to(tot_c[:, None, :], (tm // chunk, chunk, bw)).reshape(tm, bw)
    dec_out = jnp.exp(-cs)
    dec_end = jnp.exp(tot - cs)
    b = kk * a_gate
    ad_o[...] = (-kk * jnp.exp(cs - lw)).astype(ad_o.dtype)
    rd_o[...] = (r * jnp.exp(cs)).astype(rd_o.dtype)
    bd_o[...] = (b * dec_out).astype(bd_o.dtype)
    kd_o[...] = (k * dec_out).astype(kd_o.dtype)
    be_o[...] = (b * dec_end).astype(be_o.dtype)
    ke_o[...] = (k * dec_end).astype(ke_o.dtype)
    v_o[...] = v.astype(v_o.dtype)
    bonus_o[...] = _mm(r * k * rk_ref[...], ones) * v
    g_o[...] = g


def rwkv_prepare(z, mu, w0, w_up, a0, a_up, g_up, k_k, k_a, r_k, seq_len):
    n, zw = z.shape
    bw = BRANCH_WIDTH
    tm = min(ROW_TILE, seq_len)
    half = w_up.shape[0]
    zeros = jnp.zeros((LANES - half, bw), BF16)
    wup_ext = jnp.concatenate([w_up.astype(BF16), zeros], axis=0)
    aup_ext = jnp.concatenate([zeros, a_up.astype(BF16)], axis=0)
    row = lambda t: t.reshape(1, -1).astype(F32)
    tile = pl.BlockSpec((tm, bw), lambda i: (i, 0))
    cpt = tm // RWKV_CHUNK
    return pl.pallas_call(
        functools.partial(_rwkv_pre_body, tiles_per_seq=seq_len // tm),
        grid=(n // tm,),
        in_specs=[
            pl.BlockSpec((tm, zw), lambda i: (i, 0)),
            pl.BlockSpec((8, zw), lambda i: (jnp.maximum(i * (tm // 8) - 1, 0), 0)),
            _full((1, zw)), _full((1, bw)), _full((LANES, bw)), _full((1, bw)), _full((LANES, bw)),
            _full((LANES, bw)), _full((1, bw)), _full((1, bw)), _full((1, bw)),
            _resident((bw, bw)), _resident((tm, tm)), _resident((cpt, tm)),
        ],
        out_specs=[tile] * 7 + [pl.BlockSpec((cpt, 1, bw), lambda i: (i, 0, 0)), tile, tile],
        out_shape=[jax.ShapeDtypeStruct((n, bw), BF16)] * 7
        + [jax.ShapeDtypeStruct((n // RWKV_CHUNK, 1, bw), F32)] + [jax.ShapeDtypeStruct((n, bw), F32)] * 2,
        compiler_params=_params(("parallel",)),
        name="rwkv_prepare",
    )(z, z, row(mu), row(w0), wup_ext, row(a0), aup_ext, g_up.astype(BF16), row(k_k), row(k_a), row(r_k),
      _head_ones(bw, HEAD_DIM), *_chunk_selectors(tm, RWKV_CHUNK))


def _rwkv_scan_body(ad_ref, rd_ref, bd_ref, kd_ref, be_ref, ke_ref, v_ref, gam_ref, y_ref, s_ref):
    c = pl.program_id(0)

    @pl.when(c == 0)
    def _():
        s_ref[...] = jnp.zeros_like(s_ref)

    batch = v_ref.shape[0]
    chunk = RWKV_CHUNK
    n_chunks = v_ref.shape[1] // chunk
    gh, gl = RWKV_GROUP_HEADS, RWKV_GROUP_LANES
    rows = gh * chunk
    row_head = lax.broadcasted_iota(jnp.int32, (rows, gl), 0) // chunk
    lane_head = lax.broadcasted_iota(jnp.int32, (rows, gl), 1) // HEAD_DIM
    own = row_head == lane_head
    ii = lax.broadcasted_iota(jnp.int32, (rows, rows), 0)
    jj = lax.broadcasted_iota(jnp.int32, (rows, rows), 1)
    strict = jj < ii
    incl = jj <= ii
    eye = (ii == jj).astype(F32)

    def stack(x):
        return jnp.where(own, jnp.concatenate([x] * gh, axis=0), 0.0).astype(BF16)

    chains = [(bi, g) for bi in range(batch) for g in range(BRANCH_WIDTH // gl)]
    probs = [(c, bi, g) for c in range(n_chunks) for bi, g in chains]
    npb = range(len(probs))
    operand_refs = (ad_ref, rd_ref, bd_ref, kd_ref, be_ref, ke_ref, v_ref)
    ar, bk, v_s, be_ke, tot = [], [], [], [], []
    for c, bi, g in probs:
        rs = slice(c * chunk, (c + 1) * chunk)
        sl = slice(g * gl, (g + 1) * gl)
        ad, rd, bd, kd, be, ke, v = (ref[bi, rs, sl] for ref in operand_refs)
        ar.append(jnp.concatenate([stack(ad), stack(rd)], axis=0))
        bk.append(jnp.concatenate([stack(bd), stack(kd)], axis=0))
        v_s.append(stack(v))
        be_ke.append(jnp.concatenate([stack(be), stack(ke)], axis=0))
        tot.append(gam_ref[bi, c, :, sl])
    gram = [_mm_nt(ar[i], bk[i]) for i in npb]
    nmat = [jnp.where(strict, gram[i][:rows, :rows], 0.0).astype(BF16) for i in npb]
    m_ak = [jnp.where(strict, gram[i][:rows, rows:], 0.0).astype(BF16) for i in npb]
    m_rbk = [jnp.where(jnp.concatenate([incl, incl], axis=1), gram[i][rows:, :], 0.0).astype(BF16) for i in npb]
    tinv = [eye + nmat[i].astype(F32) for i in npb]
    for _ in range(chunk.bit_length() - 2):
        nmat = [_mm(nmat[i], nmat[i]).astype(BF16) for i in npb]
        tinv = [tinv[i] + _mm(tinv[i], nmat[i]) for i in npb]
    tinv = [tinv[i].astype(BF16) for i in npb]
    x0 = [_mm(m_ak[i], v_s[i]) for i in npb]

    sts = [s_ref[bi, g] for bi, g in chains]
    ncn = len(chains)
    for c in range(n_chunks):
        ids = [c * ncn + j for j in range(ncn)]
        ars = [_mm_nt(ar[i], sts[j]) for j, i in enumerate(ids)]
        sa = [_mm(tinv[i], ars[j][:rows] + x0[i]).astype(BF16) for j, i in enumerate(ids)]
        sv = [jnp.concatenate([sa[j], v_s[i]], axis=0) for j, i in enumerate(ids)]
        y_s = [ars[j][rows:] + _mm(m_rbk[i], sv[j]) for j, i in enumerate(ids)]
        upd = [_mm_tn(sv[j], be_ke[i]) for j, i in enumerate(ids)]
        sts = [sts[j] * tot[i] + upd[j] for j, i in enumerate(ids)]
        for j, (bi, g) in enumerate(chains):
            y = y_s[j][0:chunk]
            for h in range(1, gh):
                y = y + y_s[j][h * chunk:(h + 1) * chunk]
            y_ref[bi, c * chunk:(c + 1) * chunk, g * gl:(g + 1) * gl] = y
    for j, (bi, g) in enumerate(chains):
        s_ref[bi, g] = sts[j]


def rwkv_scan(ad, rd, bd, kd, be, ke, v, gam):
    bsz, seq, bw = v.shape
    steps = min(RWKV_STEP_CHUNKS, seq // RWKV_CHUNK)
    span = steps * RWKV_CHUNK
    blk = pl.BlockSpec((bsz, span, bw), lambda c: (0, c, 0))
    groups = bw // RWKV_GROUP_LANES
    return pl.pallas_call(
        _rwkv_scan_body,
        grid=(seq // span,),
        in_specs=[blk] * 7 + [pl.BlockSpec((bsz, steps, 1, bw), lambda c: (0, c, 0, 0))],
        out_specs=blk,
        out_shape=jax.ShapeDtypeStruct((bsz, seq, bw), F32),
        scratch_shapes=[pltpu.VMEM((bsz, groups, RWKV_GROUP_LANES, RWKV_GROUP_LANES), F32)],
        compiler_params=_params(("arbitrary",)),
        name="rwkv_scan",
    )(ad, rd, bd, kd, be, ke, v, gam)


def _rwkv_post_body(y_ref, bonus_ref, g_ref, lnw_ref, lnb_ref, ones_ref, o_ref):
    ones = ones_ref[...]
    y = y_ref[...]
    mean = _mm(y, ones) * (1.0 / HEAD_DIM)
    yc = y - mean
    var = _mm(yc * yc, ones) * (1.0 / HEAD_DIM)
    yn = yc * lax.rsqrt(var + RWKV_GN_EPS) * lnw_ref[...] + lnb_ref[...]
    o_ref[...] = ((yn + bonus_ref[...]) * g_ref[...]).astype(o_ref.dtype)


def rwkv_finish(y, bonus, g, ln_w, ln_b):
    n, bw = y.shape
    tm = min(ROW_TILE, n)
    tile = pl.BlockSpec((tm, bw), lambda i: (i, 0))
    row = lambda t: t.reshape(1, bw).astype(F32)
    return pl.pallas_call(
        _rwkv_post_body,
        grid=(n // tm,),
        in_specs=[tile] * 3 + [_full((1, bw))] * 2 + [_resident((bw, bw))],
        out_specs=tile,
        out_shape=jax.ShapeDtypeStruct((n, bw), BF16),
        compiler_params=_params(("parallel",)),
        name="rwkv_finish",
    )(y, bonus, g, row(ln_w), row(ln_b), _head_ones(bw, HEAD_DIM))


def _hgrn_body(z_ref, lb_ref, ng_ref, o_ref, s_ref, c_ref):
    c = pl.program_id(0)

    @pl.when(c == 0)
    def _():
        s_ref[...] = jnp.zeros_like(s_ref)

    batch, rows, _ = z_ref.shape
    bw = BRANCH_WIDTH
    hd = HGRN_DIM
    ch = HGRN_CHUNK
    half = ch // 2
    upto = (lax.broadcasted_iota(jnp.int32, (ch, ch), 1) <= lax.broadcasted_iota(jnp.int32, (ch, ch), 0))
    row8 = lax.broadcasted_iota(jnp.int32, (half, hd), 0)
    lane8 = lax.broadcasted_iota(jnp.int32, (half, hd), 1)
    chains = [(bi, h) for bi in range(batch) for h in range(HGRN_HEADS)]

    def gates(rs, slot):
        lb = lb_ref[...]
        q, qe, ke, gam, og, b2, val = [], [], [], [], [], [], []
        for bi in range(batch):
            zq = z_ref[bi, rs, 0:bw]
            zf = z_ref[bi, rs, bw:2 * bw]
            zo = z_ref[bi, rs, 3 * bw:4 * bw]
            forget = lb + (1.0 - lb) * jax.nn.sigmoid(zf)
            key = 1.0 - forget
            b = _sel_mm(upto, jnp.log(forget))
            tot = b[ch - 1:ch, :]
            qb = zq * jax.nn.sigmoid(zq)
            q.append(qb)
            qe.append((qb * jnp.exp(b)).astype(BF16))
            ke.append((key * jnp.exp(tot - b)).astype(BF16))
            gam.append(jnp.exp(tot))
            og.append(zo * jax.nn.sigmoid(zo))
            val.append(z_ref[bi, rs, 2 * bw:3 * bw].astype(BF16))
            b2.append(b * LOG2_E)
            c_ref[slot, bi] = (b - jnp.log(key)) * LOG2_E
        return q, qe, ke, gam, og, b2, val

    def chunk_weights(q, b2, slot):
        attn = []
        for bi, h in chains:
            hs = slice(h * hd, (h + 1) * hd)
            b_lo = b2[bi][0:half, hs]
            b_hi = b2[bi][half:ch, hs]
            q_lo = q[bi][0:half, hs]
            q_hi = q[bi][half:ch, hs]
            a_lo = jnp.zeros((half, hd), F32)
            a_hi = jnp.zeros((half, hd), F32)
            for s in range(ch):
                c_s = c_ref[slot, bi, s:s + 1, hs]
                if s < half:
                    w = q_lo * jnp.exp2(jnp.where(row8 >= s, b_lo - c_s, MASK_VALUE))
                    a_lo = jnp.where(lane8 == s, jnp.sum(w, axis=-1, keepdims=True), a_lo)
                    w = q_hi * jnp.exp2(b_hi - c_s)
                else:
                    w = q_hi * jnp.exp2(jnp.where(row8 >= s - half, b_hi - c_s, MASK_VALUE))
                a_hi = jnp.where(lane8 == s, jnp.sum(w, axis=-1, keepdims=True), a_hi)
            attn.append(jnp.concatenate([a_lo, a_hi], axis=0)[:, 0:ch].astype(BF16))
        return attn

    def chunk_group(jj, carry):
        slots = range(HGRN_UNROLL)
        rss = [pl.ds(pl.multiple_of((jj * HGRN_UNROLL + u) * ch, ch), ch) for u in slots]
        gated = [gates(rss[u], u) for u in slots]
        attn = [chunk_weights(gated[u][0], gated[u][5], u) for u in slots]
        sts = [s_ref[bi, h] for bi, h in chains]
        outs = []
        for u in slots:
            _, qe, ke, gam, og, _, val = gated[u]
            hsl = [slice(h * hd, (h + 1) * hd) for _, h in chains]
            inter = [_mm_nt(qe[bi][:, hsl[i]], sts[i]) for i, (bi, _) in enumerate(chains)]
            upd = [_mm_tn(val[bi][:, hsl[i]], ke[bi][:, hsl[i]]) for i, (bi, _) in enumerate(chains)]
            intra = [_mm(attn[u][i], val[bi][:, hsl[i]]) for i, (bi, _) in enumerate(chains)]
            outs.append([_rms(inter[i] + intra[i], ng_ref[...]) * og[bi][:, hsl[i]]
                         for i, (bi, _) in enumerate(chains)])
            sts = [sts[i] * gam[bi][:, hsl[i]] + upd[i] for i, (bi, _) in enumerate(chains)]
        for u in slots:
            for i, (bi, h) in enumerate(chains):
                o_ref[bi, rss[u], h * hd:(h + 1) * hd] = outs[u][i].astype(o_ref.dtype)
        for i, (bi, h) in enumerate(chains):
            s_ref[bi, h] = sts[i]
        return carry

    lax.fori_loop(0, rows // (ch * HGRN_UNROLL), chunk_group, 0)


def hgrn_mix(z, lb, norm_g):
    bsz, seq, zw = z.shape
    bw = BRANCH_WIDTH
    rows = min(HGRN_BLOCK, seq)
    blk = pltpu.VMEM((HGRN_UNROLL, bsz, HGRN_CHUNK, bw), F32)
    return pl.pallas_call(
        _hgrn_body,
        grid=(seq // rows,),
        in_specs=[pl.BlockSpec((bsz, rows, zw), lambda c: (0, c, 0)), _full((1, bw)), _full((1, HGRN_DIM))],
        out_specs=pl.BlockSpec((bsz, rows, bw), lambda c: (0, c, 0)),
        out_shape=jax.ShapeDtypeStruct((bsz, seq, bw), BF16),
        scratch_shapes=[pltpu.VMEM((bsz, HGRN_HEADS, HGRN_DIM, HGRN_DIM), F32), blk],
        compiler_params=_params(("arbitrary",)),
        name="hgrn_mix",
    )(z, lb.reshape(1, bw).astype(F32), norm_g.reshape(1, HGRN_DIM).astype(F32))


def _merge_body(x_ref, g_ref, ya_ref, yb_ref, yc_ref, wgate_ref, wb_ref, wo_ref, o_ref):
    x = x_ref[...]
    d = x.shape[1]
    h = _rms(x, g_ref[...]).astype(BF16)
    merged = jnp.zeros(x.shape, F32)
    for n, y_ref in enumerate((ya_ref, yb_ref, yc_ref)):
        gate = jax.nn.sigmoid(jnp.dot(h, wgate_ref[:, n * d:(n + 1) * d], preferred_element_type=F32))
        merged = merged + gate * jnp.dot(y_ref[...], wb_ref[n], preferred_element_type=F32)
    o_ref[...] = x + _mm(merged, wo_ref[...])


def merge_branches(x, g, ya, yb, yc, w_gate, w_branch, w_o):
    n, d = x.shape
    bw = ya.shape[1]
    tm = min(ROW_TILE, n)
    ytile = pl.BlockSpec((tm, bw), lambda i: (i, 0))
    return pl.pallas_call(
        _merge_body,
        grid=(n // tm,),
        in_specs=[pl.BlockSpec((tm, d), lambda i: (i, 0)), _full((1, d)), ytile, ytile, ytile,
                  _resident(w_gate.shape), _resident(w_branch.shape), _resident(w_o.shape)],
        out_specs=pl.BlockSpec((tm, d), lambda i: (i, 0)),
        out_shape=jax.ShapeDtypeStruct((n, d), F32),
        compiler_params=_params(("parallel",)),
        name="merge_branches",
    )(x, g.reshape(1, d), ya, yb, yc, w_gate, w_branch, w_o)


def token_mixer(x, seq_len, mix_norm, w_in, attn_sinks, rwkv_mu, rwkv_w0, rwkv_w_up, rwkv_a0, rwkv_a_up, rwkv_g_up,
                rwkv_k_k, rwkv_k_a, rwkv_r_k, rwkv_ln_w, rwkv_ln_b, lb, hgrn_norm, w_branch, w_o):
    n, d = x.shape
    bsz = n // seq_len
    bw = BRANCH_WIDTH
    attn_w = (ATTN_HEADS + 2 * (ATTN_HEADS // 4)) * HEAD_DIM
    rwkv_w = rwkv_mu.shape[0]
    hgrn_w = 4 * bw
    o1, o2, o3 = attn_w, attn_w + rwkv_w, attn_w + rwkv_w + hgrn_w
    w_attn, w_rwkv, w_hgrn, w_gate = (w_in[:, a:b].astype(BF16) for a, b in ((0, o1), (o1, o2), (o2, o3), (o3, None)))
    z_attn, z_rwkv, z_hgrn = norm_project(x, mix_norm, (w_attn, w_rwkv, w_hgrn), (BF16, F32, F32))

    y_a = sliding_window_attention(z_attn, attn_sinks, seq_len)

    *operands, gam, bonus, g = rwkv_prepare(z_rwkv, rwkv_mu, rwkv_w0, rwkv_w_up, rwkv_a0, rwkv_a_up, rwkv_g_up,
                                            rwkv_k_k, rwkv_k_a, rwkv_r_k, seq_len)
    y = rwkv_scan(*(t.reshape(bsz, seq_len, bw) for t in operands),
                  gam.reshape(bsz, seq_len // RWKV_CHUNK, 1, bw)).reshape(n, bw)
    y_b = rwkv_finish(y, bonus, g, rwkv_ln_w, rwkv_ln_b)

    y_c = hgrn_mix(z_hgrn.reshape(bsz, seq_len, hgrn_w), lb, hgrn_norm).reshape(n, bw)

    return merge_branches(x, mix_norm, y_a, y_b, y_c, w_gate, w_branch.astype(BF16), w_o.astype(BF16))


def kernel(x, p, ffn1_norm, ffn1_w_gate, ffn1_w_up, ffn1_w_down, mix_norm, w_in, attn_sinks, rwkv_mu, rwkv_w0,
           rwkv_w_up, rwkv_a0, rwkv_a_up, rwkv_g_up, rwkv_k_k, rwkv_k_a, rwkv_r_k, rwkv_ln_w, rwkv_ln_b, hgrn_lb,
           hgrn_norm, w_branch, w_o, ffn2_norm, ffn2_w_gate, ffn2_w_up, ffn2_w_down, ple_norm, ple_w_gate,
           ple_w_proj, final_norm):
    bsz, seq_len, d = x.shape
    depth = p.shape[0]
    n = bsz * seq_len
    lb_all = jax.nn.softmax(hgrn_lb.astype(F32), axis=0)
    lb_layers = jnp.clip(jnp.cumsum(lb_all, axis=0) - lb_all[0:1], 0.0, 1.0 - 1e-6)
    bf = lambda t: t.astype(BF16)
    x = x.reshape(n, d)
    for i in range(depth):
        x = ffn_half_step(x, ffn1_norm[i], bf(ffn1_w_gate[i]), bf(ffn1_w_up[i]), bf(ffn1_w_down[i]))
        x = token_mixer(x, seq_len, mix_norm[i], w_in[i], attn_sinks[i], rwkv_mu[i], rwkv_w0[i], rwkv_w_up[i],
                        rwkv_a0[i], rwkv_a_up[i], rwkv_g_up[i], rwkv_k_k[i], rwkv_k_a[i], rwkv_r_k[i],
                        rwkv_ln_w[i], rwkv_ln_b[i], lb_layers[i], hgrn_norm[i], w_branch[i], w_o[i])
        ple = (p[i].reshape(n, -1), ple_norm[i], bf(ple_w_gate[i]), bf(ple_w_proj[i]), final_norm)
        x = ffn_half_step(x, ffn2_norm[i], bf(ffn2_w_gate[i]), bf(ffn2_w_up[i]), bf(ffn2_w_down[i]), ple=ple,
                          final=(i == depth - 1))
    return x.reshape(bsz, seq_len, d)
```

```python
import functools

import jax
import jax.numpy as jnp
from jax import lax
from jax.experimental import pallas as pl
from jax.experimental.pallas import tpu as pltpu

F32 = jnp.float32
BF16 = jnp.bfloat16

NORM_EPS = 1e-6
FFN_HALF = 0.5
MASK_VALUE = -1e30
HEAD_DIM = 64
ATTN_HEADS = 8
ATTN_BLOCK = 128
ATTN_SCALE = HEAD_DIM ** -0.5
assert ATTN_SCALE == 0.125
BRANCH_WIDTH = 512
RWKV_GN_EPS = 64e-5
RWKV_DECAY_SCALE = 0.6065306597126334
LOG2_E = 1.4426950408889634
HGRN_HEADS = 4
HGRN_DIM = 128

LANES = 128
VMEM_LIMIT_BYTES = 56 * 1024 * 1024

FFN_ROWS = 512
FFN_CHUNKS = 11
ROW_TILE = 512
WEIGHT_CHUNK = 256
ATTN_STEP_BLOCKS = 4
RWKV_CHUNK = 32
RWKV_STEP_CHUNKS = 4
RWKV_GROUP_HEADS = 4
RWKV_GROUP_LANES = RWKV_GROUP_HEADS * HEAD_DIM
HGRN_CHUNK = 16
HGRN_BLOCK = 256
HGRN_UNROLL = 8


def _params(semantics):
    return pltpu.CompilerParams(dimension_semantics=semantics, vmem_limit_bytes=VMEM_LIMIT_BYTES)


def _mm(a, b):
    return jnp.dot(a.astype(BF16), b.astype(BF16), preferred_element_type=F32)


def _mm_nt(a, b):
    return lax.dot_general(a.astype(BF16), b.astype(BF16), (((1,), (1,)), ((), ())), preferred_element_type=F32)


def _mm_tn(a, b):
    return lax.dot_general(a.astype(BF16), b.astype(BF16), (((0,), (0,)), ((), ())), preferred_element_type=F32)


def _split(x, pieces):
    out = []
    for _ in range(pieces - 1):
        p = x.astype(BF16)
        out.append(p)
        x = x - p.astype(F32)
    out.append(x.astype(BF16))
    return out


def _sel_mm(sel, x, pieces=3):
    sel = sel.astype(BF16)
    return sum(jnp.dot(sel, p, preferred_element_type=F32) for p in _split(x, pieces))


def _rms(x, g):
    ms = jnp.mean(x * x, axis=-1, keepdims=True)
    return x * lax.rsqrt(ms + NORM_EPS) * g


def _full(shape):
    return pl.BlockSpec(shape, lambda *_: (0,) * len(shape))


def _resident(shape):
    return pl.BlockSpec(shape, lambda *_: (0,) * len(shape), pipeline_mode=pl.Buffered(1))


def _ffn_body(x_ref, g_ref, wg_ref, wu_ref, wd_ref, *rest, chunks, with_ple, final):
    i = pl.program_id(0)
    if with_ple:
        p_ref, pg_ref, pwg_ref, pwp_ref, gf_ref, o_ref, wg_s, wu_s, wd_s, pwg_s, pwp_s = rest
    else:
        o_ref, wg_s, wu_s, wd_s = rest

    @pl.when(i < chunks)
    def _():
        wg_s[i] = wg_ref[...].astype(BF16)
        wu_s[i] = wu_ref[...].astype(BF16)
        wd_s[i] = wd_ref[...].astype(BF16)
        if with_ple:
            @pl.when(i == 0)
            def _():
                pwg_s[...] = pwg_ref[...].astype(BF16)
                pwp_s[...] = pwp_ref[...].astype(BF16)

    @pl.when(i >= chunks)
    def _():
        x = x_ref[...]
        h = _rms(x, g_ref[...]).astype(BF16)
        acc = jnp.zeros(x.shape, F32)
        for k in range(chunks):
            gate = jnp.dot(h, wg_s[k], preferred_element_type=F32)
            up = jnp.dot(h, wu_s[k], preferred_element_type=F32)
            act = (gate * jax.nn.sigmoid(gate) * up).astype(BF16)
            acc = acc + jnp.dot(act, wd_s[k], preferred_element_type=F32)
        x = x + FFN_HALF * acc
        if with_ple:
            pgate = jax.nn.sigmoid(_mm(_rms(x, pg_ref[...]), pwg_s[...]))
            x = x + pgate * _mm(p_ref[...], pwp_s[...])
            if final:
                x = _rms(x, gf_ref[...])
        o_ref[...] = x


def ffn_half_step(x, g, wg, wu, wd, ple=None, final=False):
    n, d = x.shape
    f = wg.shape[1]
    tm = min(FFN_ROWS, n)
    chunks = FFN_CHUNKS
    cf = f // chunks
    rows = lambda i: (jnp.maximum(i - chunks, 0), 0)
    col_chunk = lambda i: (0, jnp.minimum(i, chunks - 1))
    row_chunk = lambda i: (jnp.minimum(i, chunks - 1), 0)
    row = lambda t: t.reshape(1, d)
    args = [x, row(g), wg, wu, wd]
    in_specs = [pl.BlockSpec((tm, d), rows), _full((1, d)),
                pl.BlockSpec((d, cf), col_chunk), pl.BlockSpec((d, cf), col_chunk), pl.BlockSpec((cf, d), row_chunk)]
    scratch = [pltpu.VMEM((chunks, d, cf), BF16), pltpu.VMEM((chunks, d, cf), BF16), pltpu.VMEM((chunks, cf, d), BF16)]
    if ple is not None:
        p, pg, pwg, pwp, gf = ple
        pw = p.shape[1]
        args += [p, row(pg), pwg, pwp, row(gf)]
        in_specs += [pl.BlockSpec((tm, pw), rows), _full((1, d)), _resident((d, d)), _resident((pw, d)),
                     _full((1, d))]
        scratch += [pltpu.VMEM((d, d), BF16), pltpu.VMEM((pw, d), BF16)]
    return pl.pallas_call(
        functools.partial(_ffn_body, chunks=chunks, with_ple=ple is not None, final=final),
        grid=(chunks + n // tm,),
        in_specs=in_specs,
        out_specs=pl.BlockSpec((tm, d), rows),
        out_shape=jax.ShapeDtypeStruct((n, d), F32),
        scratch_shapes=scratch,
        compiler_params=_params(("arbitrary",)),
        name="ffn_half_step",
    )(*args)


def _proj_body(x_ref, g_ref, w_ref, *rest, group_chunks):
    k = len(group_chunks)
    outs, scratch = rest[:k], rest[k:]
    total = sum(group_chunks)
    i = pl.program_id(0)
    first = 0
    for w_s, count in zip(scratch, group_chunks):
        @pl.when((i >= first) & (i < first + count))
        def _(w_s=w_s, first=first):
            w_s[i - first] = w_ref[...].astype(BF16)
        first += count

    @pl.when(i >= total)
    def _():
        h = _rms(x_ref[...], g_ref[...]).astype(BF16)
        for o_ref, w_s, count in zip(outs, scratch, group_chunks):
            for c in range(count):
                o_ref[:, c * WEIGHT_CHUNK:(c + 1) * WEIGHT_CHUNK] = jnp.dot(
                    h, w_s[c], preferred_element_type=F32).astype(o_ref.dtype)


def norm_project(x, g, w, first_col, widths, out_dtypes):
    n, d = x.shape
    tm = min(ROW_TILE, n)
    ch = WEIGHT_CHUNK
    group_chunks = tuple(wd // ch for wd in widths)
    total = sum(group_chunks)
    rows = lambda i: (jnp.maximum(i - total, 0), 0)
    return pl.pallas_call(
        functools.partial(_proj_body, group_chunks=group_chunks),
        grid=(total + n // tm,),
        in_specs=[pl.BlockSpec((tm, d), rows), _full((1, d)),
                  pl.BlockSpec((d, ch), lambda i: (0, first_col // ch + jnp.minimum(i, total - 1)))],
        out_specs=[pl.BlockSpec((tm, wd), rows) for wd in widths],
        out_shape=[jax.ShapeDtypeStruct((n, wd), dt) for wd, dt in zip(widths, out_dtypes)],
        scratch_shapes=[pltpu.VMEM((c, d, ch), BF16) for c in group_chunks],
        compiler_params=_params(("arbitrary",)),
        name="norm_project",
    )(x, g.reshape(1, d), w)


def _attn_body(q_ref, kc_ref, kp_ref, vc_ref, vp_ref, sink_ref, o_ref, *, blocks_per_seq):
    blk = ATTN_BLOCK
    n_sub = q_ref.shape[0] // blk
    step = pl.program_id(0)
    lane = lax.broadcasted_iota(jnp.int32, (2 * blk, LANES), 1)
    low = lane < HEAD_DIM
    qi = lax.broadcasted_iota(jnp.int32, (blk, 2 * blk), 0)
    kj = lax.broadcasted_iota(jnp.int32, (blk, 2 * blk), 1)
    rel = blk + qi - kj
    band = (rel >= 0) & (rel < blk)

    def placed(x, x_sw, kv_head, low_half):
        src = x if (kv_head == 0) == low_half else x_sw
        return jnp.where(low if low_half else jnp.logical_not(low), src, 0.0).astype(BF16)

    probs = [(j, kv, half) for j in range(n_sub) for kv in range(2) for half in range(2)]
    masks, k_ext, v_ext, qs = [], {}, {}, {}
    for j in range(n_sub):
        rows = slice(j * blk, (j + 1) * blk)
        prev = slice((j - 1) * blk, j * blk)
        kb = jnp.concatenate([kp_ref[...] if j == 0 else kc_ref[prev, :], kc_ref[rows, :]], axis=0).astype(F32)
        vb = jnp.concatenate([vp_ref[...] if j == 0 else vc_ref[prev, :], vc_ref[rows, :]], axis=0).astype(F32)
        kb_sw = pltpu.roll(kb, HEAD_DIM, 1)
        vb_sw = pltpu.roll(vb, HEAD_DIM, 1)
        first = ((step * n_sub + j) % blocks_per_seq == 0).astype(jnp.int32)
        masks.append(band & (kj >= first * blk))
        for kv in range(2):
            qs[j, kv] = ATTN_SCALE * jnp.concatenate([q_ref[rows, (2 * kv) * LANES:(2 * kv + 1) * LANES],
                                                      q_ref[rows, (2 * kv + 1) * LANES:(2 * kv + 2) * LANES]], axis=0)
            for half in range(2):
                k_ext[j, kv, half] = placed(kb, kb_sw, kv, half == 0)
                v_ext[j, kv, half] = placed(vb, vb_sw, kv, half == 0)
    scores = [_mm_nt(qs[j, kv], k_ext[j, kv, half]) for j, kv, half in probs]
    sinks = [jnp.max(sink_ref[h:h + 1, :], axis=-1, keepdims=True) for h in range(ATTN_HEADS)]
    weights = []
    for i, (j, kv, half) in enumerate(probs):
        parts = []
        for pair in range(2):
            sink = sinks[4 * kv + 2 * pair + half]
            s = jnp.where(masks[j], scores[i][pair * blk:(pair + 1) * blk], MASK_VALUE)
            m = jnp.maximum(jnp.max(s, axis=-1, keepdims=True), sink)
            e = jnp.exp(s - m)
            denom = jnp.sum(e, axis=-1, keepdims=True) + jnp.exp(sink - m)
            parts.append((e * (1.0 / denom)).astype(BF16))
        weights.append(jnp.concatenate(parts, axis=0))
    outs = [_mm(weights[i], v_ext[p]) for i, p in enumerate(probs)]
    for j in range(n_sub):
        for kv in range(2):
            i = (j * 2 + kv) * 2
            out = outs[i] + outs[i + 1]
            rows = slice(j * blk, (j + 1) * blk)
            o_ref[rows, (2 * kv) * LANES:(2 * kv + 1) * LANES] = out[:blk].astype(o_ref.dtype)
            o_ref[rows, (2 * kv + 1) * LANES:(2 * kv + 2) * LANES] = out[blk:].astype(o_ref.dtype)


def sliding_window_attention(z_attn, sinks, seq_len):
    n = z_attn.shape[0]
    blk = ATTN_BLOCK
    span = min(ATTN_STEP_BLOCKS * blk, seq_len)
    sub = span // blk
    qw = ATTN_HEADS * HEAD_DIM
    kcol = qw // LANES
    vcol = kcol + 1
    sink_rows = jnp.broadcast_to(sinks.astype(F32)[:, None], (ATTN_HEADS, LANES))
    prev = lambda i: jnp.maximum(i * sub - 1, 0)
    return pl.pallas_call(
        functools.partial(_attn_body, blocks_per_seq=seq_len // blk),
        grid=(n // span,),
        in_specs=[
            pl.BlockSpec((span, qw), lambda i: (i, 0)),
            pl.BlockSpec((span, LANES), lambda i: (i, kcol)),
            pl.BlockSpec((blk, LANES), lambda i: (prev(i), kcol)),
            pl.BlockSpec((span, LANES), lambda i: (i, vcol)),
            pl.BlockSpec((blk, LANES), lambda i: (prev(i), vcol)),
            _full((ATTN_HEADS, LANES)),
        ],
        out_specs=pl.BlockSpec((span, qw), lambda i: (i, 0)),
        out_shape=jax.ShapeDtypeStruct((n, qw), BF16),
        compiler_params=_params(("parallel",)),
        name="sliding_window_attention",
    )(z_attn, z_attn, z_attn, z_attn, z_attn, sink_rows)


def _head_ones(width, head):
    idx = jnp.arange(width) // head
    return (idx[:, None] == idx[None, :]).astype(BF16)


def _chunk_selectors(rows, chunk):
    r = jnp.arange(rows)
    prefix = (r[:, None] // chunk == r[None, :] // chunk) & (r[None, :] <= r[:, None])
    member = jnp.arange(rows // chunk)[:, None] == r[None, :] // chunk
    return prefix.astype(BF16), member.astype(BF16)


def _rwkv_pre_body(z_ref, zp_ref, mu_ref, w0_ref, wup_ref, a0_ref, aup_ref, gup_ref, kk_ref, ka_ref, rk_ref,
                   ones_ref, prefix_ref, member_ref,
                   ad_o, rd_o, bd_o, kd_o, be_o, ke_o, v_o, gam_o, bonus_o, g_o, *, tiles_per_seq):
    bw = BRANCH_WIDTH
    i = pl.program_id(0)
    z = z_ref[...]
    keep = (i % tiles_per_seq != 0).astype(F32)
    prev_row = zp_ref[7:8, :] * keep
    row = lax.broadcasted_iota(jnp.int32, z.shape, 0)
    z_prev = jnp.where(row == 0, prev_row, pltpu.roll(z, 1, 0))
    z = z + mu_ref[...] * (z_prev - z)
    r = z[:, 0:bw]
    k = z[:, bw:2 * bw]
    v = z[:, 2 * bw:3 * bw]
    x_wa = z[:, 3 * bw:3 * bw + LANES]
    x_g = z[:, 3 * bw + LANES:3 * bw + 2 * LANES]
    d = w0_ref[...] + _mm(jnp.tanh(x_wa), wup_ref[...])
    lw = -RWKV_DECAY_SCALE * jax.nn.sigmoid(d)
    a_gate = jax.nn.sigmoid(a0_ref[...] + _mm(x_wa, aup_ref[...]))
    g = _mm(jax.nn.sigmoid(x_g), gup_ref[...])
    kk = k * kk_ref[...]
    ones = ones_ref[...]
    kk = kk * lax.rsqrt(jnp.maximum(_mm(kk * kk, ones), 1e-24))
    k = k * (1.0 + (a_gate - 1.0) * ka_ref[...])
    chunk = RWKV_CHUNK
    tm = z.shape[0]
    cs = _sel_mm(prefix_ref[...], lw, pieces=2)
    tot_c = _sel_mm(member_ref[...], lw)
    gam_o[:, 0, :] = jnp.exp(tot_c)
    tot = jnp.broadcast_to(tot_c[:, None, :], (tm // chunk, chunk, bw)).reshape(tm, bw)
    dec_out = jnp.exp(-cs)
    dec_end = jnp.exp(tot - cs)
    b = kk * a_gate
    ad_o[...] = (-kk * jnp.exp(cs - lw)).astype(ad_o.dtype)
    rd_o[...] = (r * jnp.exp(cs)).astype(rd_o.dtype)
    bd_o[...] = (b * dec_out).astype(bd_o.dtype)
    kd_o[...] = (k * dec_out).astype(kd_o.dtype)
    be_o[...] = (b * dec_end).astype(be_o.dtype)
    ke_o[...] = (k * dec_end).astype(ke_o.dtype)
    v_o[...] = v.astype(v_o.dtype)
    bonus_o[...] = _mm(r * k * rk_ref[...], ones) * v
    g_o[...] = g


def rwkv_prepare(z, mu, w0, w_up, a0, a_up, g_up, k_k, k_a, r_k, seq_len):
    n, zw = z.shape
    bw = BRANCH_WIDTH
    tm = min(ROW_TILE, seq_len)
    half = w_up.shape[0]
    zeros = jnp.zeros((LANES - half, bw), BF16)
    wup_ext = jnp.concatenate([w_up.astype(BF16), zeros], axis=0)
    aup_ext = jnp.concatenate([zeros, a_up.astype(BF16)], axis=0)
    row = lambda t: t.reshape(1, -1).astype(F32)
    tile = pl.BlockSpec((tm, bw), lambda i: (i, 0))
    cpt = tm // RWKV_CHUNK
    return pl.pallas_call(
        functools.partial(_rwkv_pre_body, tiles_per_seq=seq_len // tm),
        grid=(n // tm,),
        in_specs=[
            pl.BlockSpec((tm, zw), lambda i: (i, 0)),
            pl.BlockSpec((8, zw), lambda i: (jnp.maximum(i * (tm // 8) - 1, 0), 0)),
            _full((1, zw)), _full((1, bw)), _full((LANES, bw)), _full((1, bw)), _full((LANES, bw)),
            _full((LANES, bw)), _full((1, bw)), _full((1, bw)), _full((1, bw)),
            _resident((bw, bw)), _resident((tm, tm)), _resident((cpt, tm)),
        ],
        out_specs=[tile] * 7 + [pl.BlockSpec((cpt, 1, bw), lambda i: (i, 0, 0)), tile, tile],
        out_shape=[jax.ShapeDtypeStruct((n, bw), BF16)] * 7
        + [jax.ShapeDtypeStruct((n // RWKV_CHUNK, 1, bw), F32)] + [jax.ShapeDtypeStruct((n, bw), F32)] * 2,
        compiler_params=_params(("parallel",)),
        name="rwkv_prepare",
    )(z, z, row(mu), row(w0), wup_ext, row(a0), aup_ext, g_up.astype(BF16), row(k_k), row(k_a), row(r_k),
      _head_ones(bw, HEAD_DIM), *_chunk_selectors(tm, RWKV_CHUNK))


def _rwkv_scan_body(ad_ref, rd_ref, bd_ref, kd_ref, be_ref, ke_ref, v_ref, gam_ref, y_ref, s_ref):
    c = pl.program_id(0)

    @pl.when(c == 0)
    def _():
        s_ref[...] = jnp.zeros_like(s_ref)

    batch = v_ref.shape[0]
    chunk = RWKV_CHUNK
    n_chunks = v_ref.shape[1] // chunk
    gh, gl = RWKV_GROUP_HEADS, RWKV_GROUP_LANES
    rows = gh * chunk
    row_head = lax.broadcasted_iota(jnp.int32, (rows, gl), 0) // chunk
    lane_head = lax.broadcasted_iota(jnp.int32, (rows, gl), 1) // HEAD_DIM
    own = row_head == lane_head
    ii = lax.broadcasted_iota(jnp.int32, (rows, rows), 0)
    jj = lax.broadcasted_iota(jnp.int32, (rows, rows), 1)
    strict = jj < ii
    incl = jj <= ii
    eye = (ii == jj).astype(F32)

    def stack(x):
        return jnp.where(own, jnp.concatenate([x] * gh, axis=0), 0.0).astype(BF16)

    chains = [(bi, g) for bi in range(batch) for g in range(BRANCH_WIDTH // gl)]
    probs = [(c, bi, g) for c in range(n_chunks) for bi, g in chains]
    npb = range(len(probs))
    operand_refs = (ad_ref, rd_ref, bd_ref, kd_ref, be_ref, ke_ref, v_ref)
    ar, bk, v_s, be_ke, tot = [], [], [], [], []
    for c, bi, g in probs:
        rs = slice(c * chunk, (c + 1) * chunk)
        sl = slice(g * gl, (g + 1) * gl)
        ad, rd, bd, kd, be, ke, v = (ref[bi, rs, sl] for ref in operand_refs)
        ar.append(jnp.concatenate([stack(ad), stack(rd)], axis=0))
        bk.append(jnp.concatenate([stack(bd), stack(kd)], axis=0))
        v_s.append(stack(v))
        be_ke.append(jnp.concatenate([stack(be), stack(ke)], axis=0))
        tot.append(gam_ref[bi, c, :, sl])
    gram = [_mm_nt(ar[i], bk[i]) for i in npb]
    nmat = [jnp.where(strict, gram[i][:rows, :rows], 0.0).astype(BF16) for i in npb]
    m_ak = [jnp.where(strict, gram[i][:rows, rows:], 0.0).astype(BF16) for i in npb]
    m_rbk = [jnp.where(jnp.concatenate([incl, incl], axis=1), gram[i][rows:, :], 0.0).astype(BF16) for i in npb]
    tinv = [eye + nmat[i].astype(F32) for i in npb]
    for _ in range(chunk.bit_length() - 2):
        nmat = [_mm(nmat[i], nmat[i]).astype(BF16) for i in npb]
        tinv = [tinv[i] + _mm(tinv[i], nmat[i]) for i in npb]
    tinv = [tinv[i].astype(BF16) for i in npb]
    x0 = [_mm(m_ak[i], v_s[i]) for i in npb]

    sts = [s_ref[bi, g] for bi, g in chains]
    ncn = len(chains)
    for c in range(n_chunks):
        ids = [c * ncn + j for j in range(ncn)]
        ars = [_mm_nt(ar[i], sts[j]) for j, i in enumerate(ids)]
        sa = [_mm(tinv[i], ars[j][:rows] + x0[i]).astype(BF16) for j, i in enumerate(ids)]
        sv = [jnp.concatenate([sa[j], v_s[i]], axis=0) for j, i in enumerate(ids)]
        y_s = [ars[j][rows:] + _mm(m_rbk[i], sv[j]) for j, i in enumerate(ids)]
        upd = [_mm_tn(sv[j], be_ke[i]) for j, i in enumerate(ids)]
        sts = [sts[j] * tot[i] + upd[j] for j, i in enumerate(ids)]
        for j, (bi, g) in enumerate(chains):
            y = y_s[j][0:chunk]
            for h in range(1, gh):
                y = y + y_s[j][h * chunk:(h + 1) * chunk]
            y_ref[bi, c * chunk:(c + 1) * chunk, g * gl:(g + 1) * gl] = y
    for j, (bi, g) in enumerate(chains):
        s_ref[bi, g] = sts[j]


def rwkv_scan(ad, rd, bd, kd, be, ke, v, gam):
    bsz, seq, bw = v.shape
    steps = min(RWKV_STEP_CHUNKS, seq // RWKV_CHUNK)
    span = steps * RWKV_CHUNK
    blk = pl.BlockSpec((bsz, span, bw), lambda c: (0, c, 0))
    groups = bw // RWKV_GROUP_LANES
    return pl.pallas_call(
        _rwkv_scan_body,
        grid=(seq // span,),
        in_specs=[blk] * 7 + [pl.BlockSpec((bsz, steps, 1, bw), lambda c: (0, c, 0, 0))],
        out_specs=blk,
        out_shape=jax.ShapeDtypeStruct((bsz, seq, bw), F32),
        scratch_shapes=[pltpu.VMEM((bsz, groups, RWKV_GROUP_LANES, RWKV_GROUP_LANES), F32)],
        compiler_params=_params(("arbitrary",)),
        name="rwkv_scan",
    )(ad, rd, bd, kd, be, ke, v, gam)


def _rwkv_post_body(y_ref, bonus_ref, g_ref, lnw_ref, lnb_ref, ones_ref, o_ref):
    ones = ones_ref[...]
    y = y_ref[...]
    mean = _mm(y, ones) * (1.0 / HEAD_DIM)
    yc = y - mean
    var = _mm(yc * yc, ones) * (1.0 / HEAD_DIM)
    yn = yc * lax.rsqrt(var + RWKV_GN_EPS) * lnw_ref[...] + lnb_ref[...]
    o_ref[...] = ((yn + bonus_ref[...]) * g_ref[...]).astype(o_ref.dtype)


def rwkv_finish(y, bonus, g, ln_w, ln_b):
    n, bw = y.shape
    tm = min(ROW_TILE, n)
    tile = pl.BlockSpec((tm, bw), lambda i: (i, 0))
    row = lambda t: t.reshape(1, bw).astype(F32)
    return pl.pallas_call(
        _rwkv_post_body,
        grid=(n // tm,),
        in_specs=[tile] * 3 + [_full((1, bw))] * 2 + [_resident((bw, bw))],
        out_specs=tile,
        out_shape=jax.ShapeDtypeStruct((n, bw), BF16),
        compiler_params=_params(("parallel",)),
        name="rwkv_finish",
    )(y, bonus, g, row(ln_w), row(ln_b), _head_ones(bw, HEAD_DIM))


def _hgrn_body(z_ref, lb_ref, ng_ref, o_ref, s_ref, c_ref):
    c = pl.program_id(0)

    @pl.when(c == 0)
    def _():
        s_ref[...] = jnp.zeros_like(s_ref)

    batch, rows, _ = z_ref.shape
    bw = BRANCH_WIDTH
    hd = HGRN_DIM
    ch = HGRN_CHUNK
    half = ch // 2
    upto = (lax.broadcasted_iota(jnp.int32, (ch, ch), 1) <= lax.broadcasted_iota(jnp.int32, (ch, ch), 0))
    row8 = lax.broadcasted_iota(jnp.int32, (half, hd), 0)
    lane8 = lax.broadcasted_iota(jnp.int32, (half, hd), 1)
    chains = [(bi, h) for bi in range(batch) for h in range(HGRN_HEADS)]

    def gates(rs, slot):
        lb = lb_ref[...]
        q, qe, ke, gam, og, b2, val = [], [], [], [], [], [], []
        for bi in range(batch):
            zq = z_ref[bi, rs, 0:bw]
            zf = z_ref[bi, rs, bw:2 * bw]
            zo = z_ref[bi, rs, 3 * bw:4 * bw]
            forget = lb + (1.0 - lb) * jax.nn.sigmoid(zf)
            key = 1.0 - forget
            b = _sel_mm(upto, jnp.log(forget))
            tot = b[ch - 1:ch, :]
            qb = zq * jax.nn.sigmoid(zq)
            q.append(qb)
            qe.append((qb * jnp.exp(b)).astype(BF16))
            ke.append((key * jnp.exp(tot - b)).astype(BF16))
            gam.append(jnp.exp(tot))
            og.append(zo * jax.nn.sigmoid(zo))
            val.append(z_ref[bi, rs, 2 * bw:3 * bw].astype(BF16))
            b2.append(b * LOG2_E)
            c_ref[slot, bi] = (b - jnp.log(key)) * LOG2_E
        return q, qe, ke, gam, og, b2, val

    def chunk_weights(q, b2, slot):
        attn = []
        for bi, h in chains:
            hs = slice(h * hd, (h + 1) * hd)
            b_lo = b2[bi][0:half, hs]
            b_hi = b2[bi][half:ch, hs]
            q_lo = q[bi][0:half, hs]
            q_hi = q[bi][half:ch, hs]
            a_lo = jnp.zeros((half, hd), F32)
            a_hi = jnp.zeros((half, hd), F32)
            for s in range(ch):
                c_s = c_ref[slot, bi, s:s + 1, hs]
                if s < half:
                    w = q_lo * jnp.exp2(jnp.where(row8 >= s, b_lo - c_s, MASK_VALUE))
                    a_lo = jnp.where(lane8 == s, jnp.sum(w, axis=-1, keepdims=True), a_lo)
                    w = q_hi * jnp.exp2(b_hi - c_s)
                else:
                    w = q_hi * jnp.exp2(jnp.where(row8 >= s - half, b_hi - c_s, MASK_VALUE))
                a_hi = jnp.where(lane8 == s, jnp.sum(w, axis=-1, keepdims=True), a_hi)
            attn.append(jnp.concatenate([a_lo, a_hi], axis=0)[:, 0:ch].astype(BF16))
        return attn

    def chunk_group(jj, carry):
        slots = range(HGRN_UNROLL)
        rss = [pl.ds(pl.multiple_of((jj * HGRN_UNROLL + u) * ch, ch), ch) for u in slots]
        gated = [gates(rss[u], u) for u in slots]
        attn = [chunk_weights(gated[u][0], gated[u][5], u) for u in slots]
        sts = [s_ref[bi, h] for bi, h in chains]
        outs = []
        for u in slots:
            _, qe, ke, gam, og, _, val = gated[u]
            hsl = [slice(h * hd, (h + 1) * hd) for _, h in chains]
            inter = [_mm_nt(qe[bi][:, hsl[i]], sts[i]) for i, (bi, _) in enumerate(chains)]
            upd = [_mm_tn(val[bi][:, hsl[i]], ke[bi][:, hsl[i]]) for i, (bi, _) in enumerate(chains)]
            intra = [_mm(attn[u][i], val[bi][:, hsl[i]]) for i, (bi, _) in enumerate(chains)]
            outs.append([_rms(inter[i] + intra[i], ng_ref[...]) * og[bi][:, hsl[i]]
                         for i, (bi, _) in enumerate(chains)])
            sts = [sts[i] * gam[bi][:, hsl[i]] + upd[i] for i, (bi, _) in enumerate(chains)]
        for u in slots:
            for i, (bi, h) in enumerate(chains):
                o_ref[bi, rss[u], h * hd:(h + 1) * hd] = outs[u][i].astype(o_ref.dtype)
        for i, (bi, h) in enumerate(chains):
            s_ref[bi, h] = sts[i]
        return carry

    lax.fori_loop(0, rows // (ch * HGRN_UNROLL), chunk_group, 0)


def hgrn_mix(z, lb, norm_g):
    bsz, seq, zw = z.shape
    bw = BRANCH_WIDTH
    rows = min(HGRN_BLOCK, seq)
    blk = pltpu.VMEM((HGRN_UNROLL, bsz, HGRN_CHUNK, bw), F32)
    return pl.pallas_call(
        _hgrn_body,
        grid=(seq // rows,),
        in_specs=[pl.BlockSpec((bsz, rows, zw), lambda c: (0, c, 0)), _full((1, bw)), _full((1, HGRN_DIM))],
        out_specs=pl.BlockSpec((bsz, rows, bw), lambda c: (0, c, 0)),
        out_shape=jax.ShapeDtypeStruct((bsz, seq, bw), BF16),
        scratch_shapes=[pltpu.VMEM((bsz, HGRN_HEADS, HGRN_DIM, HGRN_DIM), F32), blk],
        compiler_params=_params(("arbitrary",)),
        name="hgrn_mix",
    )(z, lb.reshape(1, bw).astype(F32), norm_g.reshape(1, HGRN_DIM).astype(F32))


def _merge_body(x_ref, g_ref, ya_ref, yb_ref, yc_ref, wgate_ref, wb_ref, wo_ref, o_ref, wgate_s, wb_s, wo_s, *,
                chunks):
    i = pl.program_id(0)
    n_branch, wo_chunks = wb_s.shape[0], wo_s.shape[0]

    @pl.when(i < chunks)
    def _():
        wgate_s[i] = wgate_ref[...].astype(BF16)

        @pl.when(i < n_branch)
        def _():
            wb_s[i] = wb_ref[0].astype(BF16)

        @pl.when(i < wo_chunks)
        def _():
            wo_s[i] = wo_ref[...].astype(BF16)

    @pl.when(i >= chunks)
    def _():
        x = x_ref[...]
        d = x.shape[1]
        per = d // WEIGHT_CHUNK
        h = _rms(x, g_ref[...]).astype(BF16)
        merged = jnp.zeros(x.shape, F32)
        for n, y_ref in enumerate((ya_ref, yb_ref, yc_ref)):
            gate = jnp.concatenate([jnp.dot(h, wgate_s[n * per + c], preferred_element_type=F32)
                                    for c in range(per)], axis=1)
            merged = merged + jax.nn.sigmoid(gate) * jnp.dot(y_ref[...], wb_s[n], preferred_element_type=F32)
        merged = merged.astype(BF16)
        out = x
        for c in range(wo_chunks):
            rows = slice(c * WEIGHT_CHUNK, (c + 1) * WEIGHT_CHUNK)
            out = out + jnp.dot(merged[:, rows], wo_s[c], preferred_element_type=F32)
        o_ref[...] = out


def merge_branches(x, g, ya, yb, yc, w_in, gate_col, w_branch, w_o):
    n, d = x.shape
    bw = ya.shape[1]
    nb = w_branch.shape[0]
    tm = min(ROW_TILE, n)
    ch = WEIGHT_CHUNK
    chunks = nb * d // ch
    rows = lambda i: (jnp.maximum(i - chunks, 0), 0)
    ytile = pl.BlockSpec((tm, bw), rows)
    return pl.pallas_call(
        functools.partial(_merge_body, chunks=chunks),
        grid=(chunks + n // tm,),
        in_specs=[pl.BlockSpec((tm, d), rows), _full((1, d)), ytile, ytile, ytile,
                  pl.BlockSpec((d, ch), lambda i: (0, gate_col // ch + jnp.minimum(i, chunks - 1))),
                  pl.BlockSpec((1, bw, d), lambda i: (jnp.minimum(i, nb - 1), 0, 0)),
                  pl.BlockSpec((ch, d), lambda i: (jnp.minimum(i, d // ch - 1), 0))],
        out_specs=pl.BlockSpec((tm, d), rows),
        out_shape=jax.ShapeDtypeStruct((n, d), F32),
        scratch_shapes=[pltpu.VMEM((chunks, d, ch), BF16), pltpu.VMEM((nb, bw, d), BF16),
                        pltpu.VMEM((d // ch, ch, d), BF16)],
        compiler_params=_params(("arbitrary",)),
        name="merge_branches",
    )(x, g.reshape(1, d), ya, yb, yc, w_in, w_branch, w_o)


def token_mixer(x, seq_len, mix_norm, w_in, attn_sinks, rwkv_mu, rwkv_w0, rwkv_w_up, rwkv_a0, rwkv_a_up, rwkv_g_up,
                rwkv_k_k, rwkv_k_a, rwkv_r_k, rwkv_ln_w, rwkv_ln_b, lb, hgrn_norm, w_branch, w_o):
    n, d = x.shape
    bsz = n // seq_len
    bw = BRANCH_WIDTH
    attn_w = (ATTN_HEADS + 2 * (ATTN_HEADS // 4)) * HEAD_DIM
    rwkv_w = rwkv_mu.shape[0]
    hgrn_w = 4 * bw
    o3 = attn_w + rwkv_w + hgrn_w
    z_attn, z_rwkv, z_hgrn = norm_project(x, mix_norm, w_in, 0, (attn_w, rwkv_w, hgrn_w), (BF16, F32, F32))

    y_a = sliding_window_attention(z_attn, attn_sinks, seq_len)

    *operands, gam, bonus, g = rwkv_prepare(z_rwkv, rwkv_mu, rwkv_w0, rwkv_w_up, rwkv_a0, rwkv_a_up, rwkv_g_up,
                                            rwkv_k_k, rwkv_k_a, rwkv_r_k, seq_len)
    y = rwkv_scan(*(t.reshape(bsz, seq_len, bw) for t in operands),
                  gam.reshape(bsz, seq_len // RWKV_CHUNK, 1, bw)).reshape(n, bw)
    y_b = rwkv_finish(y, bonus, g, rwkv_ln_w, rwkv_ln_b)

    y_c = hgrn_mix(z_hgrn.reshape(bsz, seq_len, hgrn_w), lb, hgrn_norm).reshape(n, bw)

    return merge_branches(x, mix_norm, y_a, y_b, y_c, w_in, o3, w_branch, w_o)


def kernel(x, p, ffn1_norm, ffn1_w_gate, ffn1_w_up, ffn1_w_down, mix_norm, w_in, attn_sinks, rwkv_mu, rwkv_w0,
           rwkv_w_up, rwkv_a0, rwkv_a_up, rwkv_g_up, rwkv_k_k, rwkv_k_a, rwkv_r_k, rwkv_ln_w, rwkv_ln_b, hgrn_lb,
           hgrn_norm, w_branch, w_o, ffn2_norm, ffn2_w_gate, ffn2_w_up, ffn2_w_down, ple_norm, ple_w_gate,
           ple_w_proj, final_norm):
    bsz, seq_len, d = x.shape
    depth = p.shape[0]
    n = bsz * seq_len
    lb_all = jax.nn.softmax(hgrn_lb.astype(F32), axis=0)
    lb_layers = jnp.clip(jnp.cumsum(lb_all, axis=0) - lb_all[0:1], 0.0, 1.0 - 1e-6)
    x = x.reshape(n, d)
    for i in range(depth):
        x = ffn_half_step(x, ffn1_norm[i], ffn1_w_gate[i], ffn1_w_up[i], ffn1_w_down[i])
        x = token_mixer(x, seq_len, mix_norm[i], w_in[i], attn_sinks[i], rwkv_mu[i], rwkv_w0[i], rwkv_w_up[i],
                        rwkv_a0[i], rwkv_a_up[i], rwkv_g_up[i], rwkv_k_k[i], rwkv_k_a[i], rwkv_r_k[i],
                        rwkv_ln_w[i], rwkv_ln_b[i], lb_layers[i], hgrn_norm[i], w_branch[i], w_o[i])
        ple = (p[i].reshape(n, -1), ple_norm[i], ple_w_gate[i], ple_w_proj[i], final_norm)
        x = ffn_half_step(x, ffn2_norm[i], ffn2_w_gate[i], ffn2_w_up[i], ffn2_w_down[i], ple=ple,
                          final=(i == depth - 1))
    return x.reshape(bsz, seq_len, d)
```

```python
import functools

import jax
import jax.numpy as jnp
from jax import lax
from jax.experimental import pallas as pl
from jax.experimental.pallas import tpu as pltpu

F32 = jnp.float32
BF16 = jnp.bfloat16

NORM_EPS = 1e-6
FFN_HALF = 0.5
MASK_VALUE = -1e30
HEAD_DIM = 64
ATTN_HEADS = 8
ATTN_BLOCK = 128
ATTN_SCALE = HEAD_DIM ** -0.5
assert ATTN_SCALE == 0.125
BRANCH_WIDTH = 512
RWKV_GN_EPS = 64e-5
RWKV_DECAY_SCALE = 0.6065306597126334
LOG2_E = 1.4426950408889634
HGRN_HEADS = 4
HGRN_DIM = 128

LANES = 128
VMEM_LIMIT_BYTES = 56 * 1024 * 1024

FFN_ROWS = 512
FFN_CHUNKS = 11
ROW_TILE = 512
WEIGHT_CHUNK = 256
ATTN_STEP_BLOCKS = 4
RWKV_CHUNK = 32
RWKV_STEP_CHUNKS = 4
RWKV_GROUP_HEADS = 4
RWKV_GROUP_LANES = RWKV_GROUP_HEADS * HEAD_DIM
HGRN_CHUNK = 16
HGRN_BLOCK = 256
HGRN_UNROLL = 8


def _params(semantics):
    return pltpu.CompilerParams(dimension_semantics=semantics, vmem_limit_bytes=VMEM_LIMIT_BYTES)


def _mm(a, b):
    return jnp.dot(a.astype(BF16), b.astype(BF16), preferred_element_type=F32)


def _mm_nt(a, b):
    return lax.dot_general(a.astype(BF16), b.astype(BF16), (((1,), (1,)), ((), ())), preferred_element_type=F32)


def _mm_tn(a, b):
    return lax.dot_general(a.astype(BF16), b.astype(BF16), (((0,), (0,)), ((), ())), preferred_element_type=F32)


def _split(x, pieces):
    out = []
    for _ in range(pieces - 1):
        p = x.astype(BF16)
        out.append(p)
        x = x - p.astype(F32)
    out.append(x.astype(BF16))
    return out


def _sel_mm(sel, x, pieces=3):
    sel = sel.astype(BF16)
    return sum(jnp.dot(sel, p, preferred_element_type=F32) for p in _split(x, pieces))


def _rms(x, g):
    ms = jnp.mean(x * x, axis=-1, keepdims=True)
    return x * lax.rsqrt(ms + NORM_EPS) * g


def _full(shape):
    return pl.BlockSpec(shape, lambda *_: (0,) * len(shape))


def _resident(shape):
    return pl.BlockSpec(shape, lambda *_: (0,) * len(shape), pipeline_mode=pl.Buffered(1))


def _layer_resident(layer, shape):
    return pl.BlockSpec((None,) + shape, lambda *_: (layer,) + (0,) * len(shape), pipeline_mode=pl.Buffered(1))


def _ffn_body(x_ref, g_ref, wg_ref, wu_ref, wd_ref, *rest, chunks, with_ple, final):
    i = pl.program_id(0)
    if with_ple:
        p_ref, pg_ref, pwg_ref, pwp_ref, gf_ref, o_ref, wg_s, wu_s, wd_s, pwg_s, pwp_s = rest
    else:
        o_ref, wg_s, wu_s, wd_s = rest

    @pl.when(i < chunks)
    def _():
        wg_s[i] = wg_ref[...].astype(BF16)
        wu_s[i] = wu_ref[...].astype(BF16)
        wd_s[i] = wd_ref[...].astype(BF16)
        if with_ple:
            @pl.when(i == 0)
            def _():
                pwg_s[...] = pwg_ref[...].astype(BF16)
                pwp_s[...] = pwp_ref[...].astype(BF16)

    @pl.when(i >= chunks)
    def _():
        x = x_ref[...]
        h = _rms(x, g_ref[...]).astype(BF16)
        acc = jnp.zeros(x.shape, F32)
        for k in range(chunks):
            gate = jnp.dot(h, wg_s[k], preferred_element_type=F32)
            up = jnp.dot(h, wu_s[k], preferred_element_type=F32)
            act = (gate * jax.nn.sigmoid(gate) * up).astype(BF16)
            acc = acc + jnp.dot(act, wd_s[k], preferred_element_type=F32)
        x = x + FFN_HALF * acc
        if with_ple:
            pgate = jax.nn.sigmoid(_mm(_rms(x, pg_ref[...]), pwg_s[...]))
            x = x + pgate * _mm(p_ref[...], pwp_s[...])
            if final:
                x = _rms(x, gf_ref[...])
        o_ref[...] = x


def ffn_half_step(x, g, wg, wu, wd, layer, ple=None, final=False):
    n, d = x.shape
    f = wg.shape[2]
    tm = min(FFN_ROWS, n)
    chunks = FFN_CHUNKS
    cf = f // chunks
    rows = lambda i: (jnp.maximum(i - chunks, 0), 0)
    col_chunk = lambda i: (layer, 0, jnp.minimum(i, chunks - 1))
    row_chunk = lambda i: (layer, jnp.minimum(i, chunks - 1), 0)
    row = lambda t: t.reshape(1, d)
    args = [x, row(g), wg, wu, wd]
    in_specs = [pl.BlockSpec((tm, d), rows), _full((1, d)), pl.BlockSpec((None, d, cf), col_chunk),
                pl.BlockSpec((None, d, cf), col_chunk), pl.BlockSpec((None, cf, d), row_chunk)]
    scratch = [pltpu.VMEM((chunks, d, cf), BF16), pltpu.VMEM((chunks, d, cf), BF16), pltpu.VMEM((chunks, cf, d), BF16)]
    if ple is not None:
        p, pg, pwg, pwp, gf = ple
        pw = p.shape[2]
        args += [p, row(pg), pwg, pwp, row(gf)]
        in_specs += [pl.BlockSpec((None, tm, pw), lambda i: (layer,) + rows(i)), _full((1, d)),
                     _layer_resident(layer, (d, d)), _layer_resident(layer, (pw, d)), _full((1, d))]
        scratch += [pltpu.VMEM((d, d), BF16), pltpu.VMEM((pw, d), BF16)]
    return pl.pallas_call(
        functools.partial(_ffn_body, chunks=chunks, with_ple=ple is not None, final=final),
        grid=(chunks + n // tm,),
        in_specs=in_specs,
        out_specs=pl.BlockSpec((tm, d), rows),
        out_shape=jax.ShapeDtypeStruct((n, d), F32),
        scratch_shapes=scratch,
        compiler_params=_params(("arbitrary",)),
        name="ffn_half_step",
    )(*args)


def _proj_body(x_ref, g_ref, w_ref, *rest, group_chunks):
    k = len(group_chunks)
    outs, scratch = rest[:k], rest[k:]
    total = sum(group_chunks)
    i = pl.program_id(0)
    first = 0
    for w_s, count in zip(scratch, group_chunks):
        @pl.when((i >= first) & (i < first + count))
        def _(w_s=w_s, first=first):
            w_s[i - first] = w_ref[...].astype(BF16)
        first += count

    @pl.when(i >= total)
    def _():
        h = _rms(x_ref[...], g_ref[...]).astype(BF16)
        for o_ref, w_s, count in zip(outs, scratch, group_chunks):
            for c in range(count):
                o_ref[:, c * WEIGHT_CHUNK:(c + 1) * WEIGHT_CHUNK] = jnp.dot(
                    h, w_s[c], preferred_element_type=F32).astype(o_ref.dtype)


def norm_project(x, g, w, layer, first_col, widths, out_dtypes):
    n, d = x.shape
    tm = min(ROW_TILE, n)
    ch = WEIGHT_CHUNK
    group_chunks = tuple(wd // ch for wd in widths)
    total = sum(group_chunks)
    rows = lambda i: (jnp.maximum(i - total, 0), 0)
    return pl.pallas_call(
        functools.partial(_proj_body, group_chunks=group_chunks),
        grid=(total + n // tm,),
        in_specs=[pl.BlockSpec((tm, d), rows), _full((1, d)),
                  pl.BlockSpec((None, d, ch), lambda i: (layer, 0, first_col // ch + jnp.minimum(i, total - 1)))],
        out_specs=[pl.BlockSpec((tm, wd), rows) for wd in widths],
        out_shape=[jax.ShapeDtypeStruct((n, wd), dt) for wd, dt in zip(widths, out_dtypes)],
        scratch_shapes=[pltpu.VMEM((c, d, ch), BF16) for c in group_chunks],
        compiler_params=_params(("arbitrary",)),
        name="norm_project",
    )(x, g.reshape(1, d), w)


def _attn_body(q_ref, kc_ref, kp_ref, vc_ref, vp_ref, sink_ref, o_ref, *, blocks_per_seq):
    blk = ATTN_BLOCK
    n_sub = q_ref.shape[0] // blk
    step = pl.program_id(0)
    lane = lax.broadcasted_iota(jnp.int32, (2 * blk, LANES), 1)
    low = lane < HEAD_DIM
    qi = lax.broadcasted_iota(jnp.int32, (blk, 2 * blk), 0)
    kj = lax.broadcasted_iota(jnp.int32, (blk, 2 * blk), 1)
    rel = blk + qi - kj
    band = (rel >= 0) & (rel < blk)

    def placed(x, x_sw, kv_head, low_half):
        src = x if (kv_head == 0) == low_half else x_sw
        return jnp.where(low if low_half else jnp.logical_not(low), src, 0.0).astype(BF16)

    probs = [(j, kv, half) for j in range(n_sub) for kv in range(2) for half in range(2)]
    masks, k_ext, v_ext, qs = [], {}, {}, {}
    for j in range(n_sub):
        rows = slice(j * blk, (j + 1) * blk)
        prev = slice((j - 1) * blk, j * blk)
        kb = jnp.concatenate([kp_ref[...] if j == 0 else kc_ref[prev, :], kc_ref[rows, :]], axis=0).astype(F32)
        vb = jnp.concatenate([vp_ref[...] if j == 0 else vc_ref[prev, :], vc_ref[rows, :]], axis=0).astype(F32)
        kb_sw = pltpu.roll(kb, HEAD_DIM, 1)
        vb_sw = pltpu.roll(vb, HEAD_DIM, 1)
        first = ((step * n_sub + j) % blocks_per_seq == 0).astype(jnp.int32)
        masks.append(band & (kj >= first * blk))
        for kv in range(2):
            qs[j, kv] = ATTN_SCALE * jnp.concatenate([q_ref[rows, (2 * kv) * LANES:(2 * kv + 1) * LANES],
                                                      q_ref[rows, (2 * kv + 1) * LANES:(2 * kv + 2) * LANES]], axis=0)
            for half in range(2):
                k_ext[j, kv, half] = placed(kb, kb_sw, kv, half == 0)
                v_ext[j, kv, half] = placed(vb, vb_sw, kv, half == 0)
    scores = [_mm_nt(qs[j, kv], k_ext[j, kv, half]) for j, kv, half in probs]
    sinks = [jnp.max(sink_ref[h:h + 1, :], axis=-1, keepdims=True) for h in range(ATTN_HEADS)]
    weights = []
    for i, (j, kv, half) in enumerate(probs):
        parts = []
        for pair in range(2):
            sink = sinks[4 * kv + 2 * pair + half]
            s = jnp.where(masks[j], scores[i][pair * blk:(pair + 1) * blk], MASK_VALUE)
            m = jnp.maximum(jnp.max(s, axis=-1, keepdims=True), sink)
            e = jnp.exp(s - m)
            denom = jnp.sum(e, axis=-1, keepdims=True) + jnp.exp(sink - m)
            parts.append((e * (1.0 / denom)).astype(BF16))
        weights.append(jnp.concatenate(parts, axis=0))
    outs = [_mm(weights[i], v_ext[p]) for i, p in enumerate(probs)]
    for j in range(n_sub):
        for kv in range(2):
            i = (j * 2 + kv) * 2
            out = outs[i] + outs[i + 1]
            rows = slice(j * blk, (j + 1) * blk)
            o_ref[rows, (2 * kv) * LANES:(2 * kv + 1) * LANES] = out[:blk].astype(o_ref.dtype)
            o_ref[rows, (2 * kv + 1) * LANES:(2 * kv + 2) * LANES] = out[blk:].astype(o_ref.dtype)


def sliding_window_attention(z_attn, sinks, seq_len):
    n = z_attn.shape[0]
    blk = ATTN_BLOCK
    span = min(ATTN_STEP_BLOCKS * blk, seq_len)
    sub = span // blk
    qw = ATTN_HEADS * HEAD_DIM
    kcol = qw // LANES
    vcol = kcol + 1
    sink_rows = jnp.broadcast_to(sinks.astype(F32)[:, None], (ATTN_HEADS, LANES))
    prev = lambda i: jnp.maximum(i * sub - 1, 0)
    return pl.pallas_call(
        functools.partial(_attn_body, blocks_per_seq=seq_len // blk),
        grid=(n // span,),
        in_specs=[
            pl.BlockSpec((span, qw), lambda i: (i, 0)),
            pl.BlockSpec((span, LANES), lambda i: (i, kcol)),
            pl.BlockSpec((blk, LANES), lambda i: (prev(i), kcol)),
            pl.BlockSpec((span, LANES), lambda i: (i, vcol)),
            pl.BlockSpec((blk, LANES), lambda i: (prev(i), vcol)),
            _full((ATTN_HEADS, LANES)),
        ],
        out_specs=pl.BlockSpec((span, qw), lambda i: (i, 0)),
        out_shape=jax.ShapeDtypeStruct((n, qw), BF16),
        compiler_params=_params(("parallel",)),
        name="sliding_window_attention",
    )(z_attn, z_attn, z_attn, z_attn, z_attn, sink_rows)


def _head_ones(width, head):
    idx = jnp.arange(width) // head
    return (idx[:, None] == idx[None, :]).astype(BF16)


def _chunk_selectors(rows, chunk):
    r = jnp.arange(rows)
    prefix = (r[:, None] // chunk == r[None, :] // chunk) & (r[None, :] <= r[:, None])
    member = jnp.arange(rows // chunk)[:, None] == r[None, :] // chunk
    return prefix.astype(BF16), member.astype(BF16)


def _rwkv_pre_body(z_ref, zp_ref, mu_ref, w0_ref, wup_ref, a0_ref, aup_ref, gup_ref, kk_ref, ka_ref, rk_ref,
                   ones_ref, prefix_ref, member_ref,
                   ad_o, rd_o, bd_o, kd_o, be_o, ke_o, v_o, gam_o, bonus_o, g_o, *, tiles_per_seq):
    bw = BRANCH_WIDTH
    i = pl.program_id(0)
    z = z_ref[...]
    keep = (i % tiles_per_seq != 0).astype(F32)
    prev_row = zp_ref[7:8, :] * keep
    row = lax.broadcasted_iota(jnp.int32, z.shape, 0)
    z_prev = jnp.where(row == 0, prev_row, pltpu.roll(z, 1, 0))
    z = z + mu_ref[...] * (z_prev - z)
    r = z[:, 0:bw]
    k = z[:, bw:2 * bw]
    v = z[:, 2 * bw:3 * bw]
    x_wa = z[:, 3 * bw:3 * bw + LANES]
    x_g = z[:, 3 * bw + LANES:3 * bw + 2 * LANES]
    d = w0_ref[...] + _mm(jnp.tanh(x_wa), wup_ref[...])
    lw = -RWKV_DECAY_SCALE * jax.nn.sigmoid(d)
    a_gate = jax.nn.sigmoid(a0_ref[...] + _mm(x_wa, aup_ref[...]))
    g = _mm(jax.nn.sigmoid(x_g), gup_ref[...])
    kk = k * kk_ref[...]
    ones = ones_ref[...]
    kk = kk * lax.rsqrt(jnp.maximum(_mm(kk * kk, ones), 1e-24))
    k = k * (1.0 + (a_gate - 1.0) * ka_ref[...])
    chunk = RWKV_CHUNK
    tm = z.shape[0]
    cs = _sel_mm(prefix_ref[...], lw, pieces=2)
    tot_c = _sel_mm(member_ref[...], lw)
    gam_o[:, 0, :] = jnp.exp(tot_c)
    tot = jnp.broadcast_to(tot_c[:, None, :], (tm // chunk, chunk, bw)).reshape(tm, bw)
    dec_out = jnp.exp(-cs)
    dec_end = jnp.exp(tot - cs)
    b = kk * a_gate
    ad_o[...] = (-kk * jnp.exp(cs - lw)).astype(ad_o.dtype)
    rd_o[...] = (r * jnp.exp(cs)).astype(rd_o.dtype)
    bd_o[...] = (b * dec_out).astype(bd_o.dtype)
    kd_o[...] = (k * dec_out).astype(kd_o.dtype)
    be_o[...] = (b * dec_end).astype(be_o.dtype)
    ke_o[...] = (k * dec_end).astype(ke_o.dtype)
    v_o[...] = v.astype(v_o.dtype)
    bonus_o[...] = _mm(r * k * rk_ref[...], ones) * v
    g_o[...] = g


def rwkv_prepare(z, mu, w0, w_up, a0, a_up, g_up, k_k, k_a, r_k, seq_len):
    n, zw = z.shape
    bw = BRANCH_WIDTH
    tm = min(ROW_TILE, seq_len)
    half = w_up.shape[0]
    zeros = jnp.zeros((LANES - half, bw), BF16)
    wup_ext = jnp.concatenate([w_up.astype(BF16), zeros], axis=0)
    aup_ext = jnp.concatenate([zeros, a_up.astype(BF16)], axis=0)
    row = lambda t: t.reshape(1, -1).astype(F32)
    tile = pl.BlockSpec((tm, bw), lambda i: (i, 0))
    cpt = tm // RWKV_CHUNK
    return pl.pallas_call(
        functools.partial(_rwkv_pre_body, tiles_per_seq=seq_len // tm),
        grid=(n // tm,),
        in_specs=[
            pl.BlockSpec((tm, zw), lambda i: (i, 0)),
            pl.BlockSpec((8, zw), lambda i: (jnp.maximum(i * (tm // 8) - 1, 0), 0)),
            _full((1, zw)), _full((1, bw)), _full((LANES, bw)), _full((1, bw)), _full((LANES, bw)),
            _full((LANES, bw)), _full((1, bw)), _full((1, bw)), _full((1, bw)),
            _resident((bw, bw)), _resident((tm, tm)), _resident((cpt, tm)),
        ],
        out_specs=[tile] * 7 + [pl.BlockSpec((cpt, 1, bw), lambda i: (i, 0, 0)), tile, tile],
        out_shape=[jax.ShapeDtypeStruct((n, bw), BF16)] * 7
        + [jax.ShapeDtypeStruct((n // RWKV_CHUNK, 1, bw), F32)] + [jax.ShapeDtypeStruct((n, bw), F32)] * 2,
        compiler_params=_params(("parallel",)),
        name="rwkv_prepare",
    )(z, z, row(mu), row(w0), wup_ext, row(a0), aup_ext, g_up.astype(BF16), row(k_k), row(k_a), row(r_k),
      _head_ones(bw, HEAD_DIM), *_chunk_selectors(tm, RWKV_CHUNK))


def _rwkv_scan_body(ad_ref, rd_ref, bd_ref, kd_ref, be_ref, ke_ref, v_ref, gam_ref, y_ref, s_ref):
    c = pl.program_id(0)

    @pl.when(c == 0)
    def _():
        s_ref[...] = jnp.zeros_like(s_ref)

    batch = v_ref.shape[0]
    chunk = RWKV_CHUNK
    n_chunks = v_ref.shape[1] // chunk
    gh, gl = RWKV_GROUP_HEADS, RWKV_GROUP_LANES
    rows = gh * chunk
    row_head = lax.broadcasted_iota(jnp.int32, (rows, gl), 0) // chunk
    lane_head = lax.broadcasted_iota(jnp.int32, (rows, gl), 1) // HEAD_DIM
    own = row_head == lane_head
    ii = lax.broadcasted_iota(jnp.int32, (rows, rows), 0)
    jj = lax.broadcasted_iota(jnp.int32, (rows, rows), 1)
    strict = jj < ii
    incl = jj <= ii
    eye = (ii == jj).astype(F32)

    def stack(x):
        return jnp.where(own, jnp.concatenate([x] * gh, axis=0), 0.0).astype(BF16)

    chains = [(bi, g) for bi in range(batch) for g in range(BRANCH_WIDTH // gl)]
    probs = [(c, bi, g) for c in range(n_chunks) for bi, g in chains]
    npb = range(len(probs))
    operand_refs = (ad_ref, rd_ref, bd_ref, kd_ref, be_ref, ke_ref, v_ref)
    ar, bk, v_s, be_ke, tot = [], [], [], [], []
    for c, bi, g in probs:
        rs = slice(c * chunk, (c + 1) * chunk)
        sl = slice(g * gl, (g + 1) * gl)
        ad, rd, bd, kd, be, ke, v = (ref[bi, rs, sl] for ref in operand_refs)
        ar.append(jnp.concatenate([stack(ad), stack(rd)], axis=0))
        bk.append(jnp.concatenate([stack(bd), stack(kd)], axis=0))
        v_s.append(stack(v))
        be_ke.append(jnp.concatenate([stack(be), stack(ke)], axis=0))
        tot.append(gam_ref[bi, c, :, sl])
    gram = [_mm_nt(ar[i], bk[i]) for i in npb]
    nmat = [jnp.where(strict, gram[i][:rows, :rows], 0.0).astype(BF16) for i in npb]
    m_ak = [jnp.where(strict, gram[i][:rows, rows:], 0.0).astype(BF16) for i in npb]
    m_rbk = [jnp.where(jnp.concatenate([incl, incl], axis=1), gram[i][rows:, :], 0.0).astype(BF16) for i in npb]
    tinv = [eye + nmat[i].astype(F32) for i in npb]
    for _ in range(chunk.bit_length() - 2):
        nmat = [_mm(nmat[i], nmat[i]).astype(BF16) for i in npb]
        tinv = [tinv[i] + _mm(tinv[i], nmat[i]) for i in npb]
    tinv = [tinv[i].astype(BF16) for i in npb]
    x0 = [_mm(m_ak[i], v_s[i]) for i in npb]

    sts = [s_ref[bi, g] for bi, g in chains]
    ncn = len(chains)
    for c in range(n_chunks):
        ids = [c * ncn + j for j in range(ncn)]
        ars = [_mm_nt(ar[i], sts[j]) for j, i in enumerate(ids)]
        sa = [_mm(tinv[i], ars[j][:rows] + x0[i]).astype(BF16) for j, i in enumerate(ids)]
        sv = [jnp.concatenate([sa[j], v_s[i]], axis=0) for j, i in enumerate(ids)]
        y_s = [ars[j][rows:] + _mm(m_rbk[i], sv[j]) for j, i in enumerate(ids)]
        upd = [_mm_tn(sv[j], be_ke[i]) for j, i in enumerate(ids)]
        sts = [sts[j] * tot[i] + upd[j] for j, i in enumerate(ids)]
        for j, (bi, g) in enumerate(chains):
            y = y_s[j][0:chunk]
            for h in range(1, gh):
                y = y + y_s[j][h * chunk:(h + 1) * chunk]
            y_ref[bi, c * chunk:(c + 1) * chunk, g * gl:(g + 1) * gl] = y
    for j, (bi, g) in enumerate(chains):
        s_ref[bi, g] = sts[j]


def rwkv_scan(ad, rd, bd, kd, be, ke, v, gam):
    bsz, seq, bw = v.shape
    steps = min(RWKV_STEP_CHUNKS, seq // RWKV_CHUNK)
    span = steps * RWKV_CHUNK
    blk = pl.BlockSpec((bsz, span, bw), lambda c: (0, c, 0))
    groups = bw // RWKV_GROUP_LANES
    return pl.pallas_call(
        _rwkv_scan_body,
        grid=(seq // span,),
        in_specs=[blk] * 7 + [pl.BlockSpec((bsz, steps, 1, bw), lambda c: (0, c, 0, 0))],
        out_specs=blk,
        out_shape=jax.ShapeDtypeStruct((bsz, seq, bw), F32),
        scratch_shapes=[pltpu.VMEM((bsz, groups, RWKV_GROUP_LANES, RWKV_GROUP_LANES), F32)],
        compiler_params=_params(("arbitrary",)),
        name="rwkv_scan",
    )(ad, rd, bd, kd, be, ke, v, gam)


def _rwkv_post_body(y_ref, bonus_ref, g_ref, lnw_ref, lnb_ref, ones_ref, o_ref):
    ones = ones_ref[...]
    y = y_ref[...]
    mean = _mm(y, ones) * (1.0 / HEAD_DIM)
    yc = y - mean
    var = _mm(yc * yc, ones) * (1.0 / HEAD_DIM)
    yn = yc * lax.rsqrt(var + RWKV_GN_EPS) * lnw_ref[...] + lnb_ref[...]
    o_ref[...] = ((yn + bonus_ref[...]) * g_ref[...]).astype(o_ref.dtype)


def rwkv_finish(y, bonus, g, ln_w, ln_b):
    n, bw = y.shape
    tm = min(ROW_TILE, n)
    tile = pl.BlockSpec((tm, bw), lambda i: (i, 0))
    row = lambda t: t.reshape(1, bw).astype(F32)
    return pl.pallas_call(
        _rwkv_post_body,
        grid=(n // tm,),
        in_specs=[tile] * 3 + [_full((1, bw))] * 2 + [_resident((bw, bw))],
        out_specs=tile,
        out_shape=jax.ShapeDtypeStruct((n, bw), BF16),
        compiler_params=_params(("parallel",)),
        name="rwkv_finish",
    )(y, bonus, g, row(ln_w), row(ln_b), _head_ones(bw, HEAD_DIM))


def _hgrn_body(z_ref, lb_ref, ng_ref, o_ref, s_ref, c_ref):
    c = pl.program_id(0)

    @pl.when(c == 0)
    def _():
        s_ref[...] = jnp.zeros_like(s_ref)

    batch, rows, _ = z_ref.shape
    bw = BRANCH_WIDTH
    hd = HGRN_DIM
    ch = HGRN_CHUNK
    half = ch // 2
    upto = (lax.broadcasted_iota(jnp.int32, (ch, ch), 1) <= lax.broadcasted_iota(jnp.int32, (ch, ch), 0))
    row8 = lax.broadcasted_iota(jnp.int32, (half, hd), 0)
    lane8 = lax.broadcasted_iota(jnp.int32, (half, hd), 1)
    chains = [(bi, h) for bi in range(batch) for h in range(HGRN_HEADS)]

    def gates(rs, slot):
        lb = lb_ref[...]
        q, qe, ke, gam, og, b2, val = [], [], [], [], [], [], []
        for bi in range(batch):
            zq = z_ref[bi, rs, 0:bw]
            zf = z_ref[bi, rs, bw:2 * bw]
            zo = z_ref[bi, rs, 3 * bw:4 * bw]
            forget = lb + (1.0 - lb) * jax.nn.sigmoid(zf)
            key = 1.0 - forget
            b = _sel_mm(upto, jnp.log(forget))
            tot = b[ch - 1:ch, :]
            qb = zq * jax.nn.sigmoid(zq)
            q.append(qb)
            qe.append((qb * jnp.exp(b)).astype(BF16))
            ke.append((key * jnp.exp(tot - b)).astype(BF16))
            gam.append(jnp.exp(tot))
            og.append(zo * jax.nn.sigmoid(zo))
            val.append(z_ref[bi, rs, 2 * bw:3 * bw].astype(BF16))
            b2.append(b * LOG2_E)
            c_ref[slot, bi] = (b - jnp.log(key)) * LOG2_E
        return q, qe, ke, gam, og, b2, val

    def chunk_weights(q, b2, slot):
        attn = []
        for bi, h in chains:
            hs = slice(h * hd, (h + 1) * hd)
            b_lo = b2[bi][0:half, hs]
            b_hi = b2[bi][half:ch, hs]
            q_lo = q[bi][0:half, hs]
            q_hi = q[bi][half:ch, hs]
            a_lo = jnp.zeros((half, hd), F32)
            a_hi = jnp.zeros((half, hd), F32)
            for s in range(ch):
                c_s = c_ref[slot, bi, s:s + 1, hs]
                if s < half:
                    w = q_lo * jnp.exp2(jnp.where(row8 >= s, b_lo - c_s, MASK_VALUE))
                    a_lo = jnp.where(lane8 == s, jnp.sum(w, axis=-1, keepdims=True), a_lo)
                    w = q_hi * jnp.exp2(b_hi - c_s)
                else:
                    w = q_hi * jnp.exp2(jnp.where(row8 >= s - half, b_hi - c_s, MASK_VALUE))
                a_hi = jnp.where(lane8 == s, jnp.sum(w, axis=-1, keepdims=True), a_hi)
            attn.append(jnp.concatenate([a_lo, a_hi], axis=0)[:, 0:ch].astype(BF16))
        return attn

    def chunk_group(jj, carry):
        slots = range(HGRN_UNROLL)
        rss = [pl.ds(pl.multiple_of((jj * HGRN_UNROLL + u) * ch, ch), ch) for u in slots]
        gated = [gates(rss[u], u) for u in slots]
        attn = [chunk_weights(gated[u][0], gated[u][5], u) for u in slots]
        sts = [s_ref[bi, h] for bi, h in chains]
        outs = []
        for u in slots:
            _, qe, ke, gam, og, _, val = gated[u]
            hsl = [slice(h * hd, (h + 1) * hd) for _, h in chains]
            inter = [_mm_nt(qe[bi][:, hsl[i]], sts[i]) for i, (bi, _) in enumerate(chains)]
            upd = [_mm_tn(val[bi][:, hsl[i]], ke[bi][:, hsl[i]]) for i, (bi, _) in enumerate(chains)]
            intra = [_mm(attn[u][i], val[bi][:, hsl[i]]) for i, (bi, _) in enumerate(chains)]
            outs.append([_rms(inter[i] + intra[i], ng_ref[...]) * og[bi][:, hsl[i]]
                         for i, (bi, _) in enumerate(chains)])
            sts = [sts[i] * gam[bi][:, hsl[i]] + upd[i] for i, (bi, _) in enumerate(chains)]
        for u in slots:
            for i, (bi, h) in enumerate(chains):
                o_ref[bi, rss[u], h * hd:(h + 1) * hd] = outs[u][i].astype(o_ref.dtype)
        for i, (bi, h) in enumerate(chains):
            s_ref[bi, h] = sts[i]
        return carry

    lax.fori_loop(0, rows // (ch * HGRN_UNROLL), chunk_group, 0)


def hgrn_mix(z, lb, norm_g):
    bsz, seq, zw = z.shape
    bw = BRANCH_WIDTH
    rows = min(HGRN_BLOCK, seq)
    blk = pltpu.VMEM((HGRN_UNROLL, bsz, HGRN_CHUNK, bw), F32)
    return pl.pallas_call(
        _hgrn_body,
        grid=(seq // rows,),
        in_specs=[pl.BlockSpec((bsz, rows, zw), lambda c: (0, c, 0)), _full((1, bw)), _full((1, HGRN_DIM))],
        out_specs=pl.BlockSpec((bsz, rows, bw), lambda c: (0, c, 0)),
        out_shape=jax.ShapeDtypeStruct((bsz, seq, bw), BF16),
        scratch_shapes=[pltpu.VMEM((bsz, HGRN_HEADS, HGRN_DIM, HGRN_DIM), F32), blk],
        compiler_params=_params(("arbitrary",)),
        name="hgrn_mix",
    )(z, lb.reshape(1, bw).astype(F32), norm_g.reshape(1, HGRN_DIM).astype(F32))


def _merge_body(x_ref, g_ref, ya_ref, yb_ref, yc_ref, wgate_ref, wb_ref, wo_ref, o_ref, wgate_s, wb_s, wo_s, *,
                chunks):
    i = pl.program_id(0)
    n_branch, wo_chunks = wb_s.shape[0], wo_s.shape[0]

    @pl.when(i < chunks)
    def _():
        wgate_s[i] = wgate_ref[...].astype(BF16)

        @pl.when(i < n_branch)
        def _():
            wb_s[i] = wb_ref[0].astype(BF16)

        @pl.when(i < wo_chunks)
        def _():
            wo_s[i] = wo_ref[...].astype(BF16)

    @pl.when(i >= chunks)
    def _():
        x = x_ref[...]
        d = x.shape[1]
        per = d // WEIGHT_CHUNK
        h = _rms(x, g_ref[...]).astype(BF16)
        merged = jnp.zeros(x.shape, F32)
        for n, y_ref in enumerate((ya_ref, yb_ref, yc_ref)):
            gate = jnp.concatenate([jnp.dot(h, wgate_s[n * per + c], preferred_element_type=F32)
                                    for c in range(per)], axis=1)
            merged = merged + jax.nn.sigmoid(gate) * jnp.dot(y_ref[...], wb_s[n], preferred_element_type=F32)
        merged = merged.astype(BF16)
        out = x
        for c in range(wo_chunks):
            rows = slice(c * WEIGHT_CHUNK, (c + 1) * WEIGHT_CHUNK)
            out = out + jnp.dot(merged[:, rows], wo_s[c], preferred_element_type=F32)
        o_ref[...] = out


def merge_branches(x, g, ya, yb, yc, w_in, gate_col, w_branch, w_o, layer):
    n, d = x.shape
    bw = ya.shape[1]
    nb = w_branch.shape[1]
    tm = min(ROW_TILE, n)
    ch = WEIGHT_CHUNK
    chunks = nb * d // ch
    rows = lambda i: (jnp.maximum(i - chunks, 0), 0)
    ytile = pl.BlockSpec((tm, bw), rows)
    return pl.pallas_call(
        functools.partial(_merge_body, chunks=chunks),
        grid=(chunks + n // tm,),
        in_specs=[pl.BlockSpec((tm, d), rows), _full((1, d)), ytile, ytile, ytile,
                  pl.BlockSpec((None, d, ch), lambda i: (layer, 0, gate_col // ch + jnp.minimum(i, chunks - 1))),
                  pl.BlockSpec((None, 1, bw, d), lambda i: (layer, jnp.minimum(i, nb - 1), 0, 0)),
                  pl.BlockSpec((None, ch, d), lambda i: (layer, jnp.minimum(i, d // ch - 1), 0))],
        out_specs=pl.BlockSpec((tm, d), rows),
        out_shape=jax.ShapeDtypeStruct((n, d), F32),
        scratch_shapes=[pltpu.VMEM((chunks, d, ch), BF16), pltpu.VMEM((nb, bw, d), BF16),
                        pltpu.VMEM((d // ch, ch, d), BF16)],
        compiler_params=_params(("arbitrary",)),
        name="merge_branches",
    )(x, g.reshape(1, d), ya, yb, yc, w_in, w_branch, w_o)


def token_mixer(x, seq_len, layer, mix_norm, w_in, attn_sinks, rwkv_mu, rwkv_w0, rwkv_w_up, rwkv_a0, rwkv_a_up, rwkv_g_up,
                rwkv_k_k, rwkv_k_a, rwkv_r_k, rwkv_ln_w, rwkv_ln_b, lb, hgrn_norm, w_branch, w_o):
    n, d = x.shape
    bsz = n // seq_len
    bw = BRANCH_WIDTH
    attn_w = (ATTN_HEADS + 2 * (ATTN_HEADS // 4)) * HEAD_DIM
    rwkv_w = rwkv_mu.shape[0]
    hgrn_w = 4 * bw
    o3 = attn_w + rwkv_w + hgrn_w
    z_attn, z_rwkv, z_hgrn = norm_project(x, mix_norm, w_in, layer, 0, (attn_w, rwkv_w, hgrn_w), (BF16, F32, F32))

    y_a = sliding_window_attention(z_attn, attn_sinks, seq_len)

    *operands, gam, bonus, g = rwkv_prepare(z_rwkv, rwkv_mu, rwkv_w0, rwkv_w_up, rwkv_a0, rwkv_a_up, rwkv_g_up,
                                            rwkv_k_k, rwkv_k_a, rwkv_r_k, seq_len)
    y = rwkv_scan(*(t.reshape(bsz, seq_len, bw) for t in operands),
                  gam.reshape(bsz, seq_len // RWKV_CHUNK, 1, bw)).reshape(n, bw)
    y_b = rwkv_finish(y, bonus, g, rwkv_ln_w, rwkv_ln_b)

    y_c = hgrn_mix(z_hgrn.reshape(bsz, seq_len, hgrn_w), lb, hgrn_norm).reshape(n, bw)

    return merge_branches(x, mix_norm, y_a, y_b, y_c, w_in, o3, w_branch, w_o, layer)


def kernel(x, p, ffn1_norm, ffn1_w_gate, ffn1_w_up, ffn1_w_down, mix_norm, w_in, attn_sinks, rwkv_mu, rwkv_w0,
           rwkv_w_up, rwkv_a0, rwkv_a_up, rwkv_g_up, rwkv_k_k, rwkv_k_a, rwkv_r_k, rwkv_ln_w, rwkv_ln_b, hgrn_lb,
           hgrn_norm, w_branch, w_o, ffn2_norm, ffn2_w_gate, ffn2_w_up, ffn2_w_down, ple_norm, ple_w_gate,
           ple_w_proj, final_norm):
    bsz, seq_len, d = x.shape
    depth = p.shape[0]
    n = bsz * seq_len
    lb_all = jax.nn.softmax(hgrn_lb.astype(F32), axis=0)
    lb_layers = jnp.clip(jnp.cumsum(lb_all, axis=0) - lb_all[0:1], 0.0, 1.0 - 1e-6)
    x = x.reshape(n, d)
    for i in range(depth):
        x = ffn_half_step(x, ffn1_norm[i], ffn1_w_gate, ffn1_w_up, ffn1_w_down, i)
        x = token_mixer(x, seq_len, i, mix_norm[i], w_in, attn_sinks[i], rwkv_mu[i], rwkv_w0[i], rwkv_w_up[i],
                        rwkv_a0[i], rwkv_a_up[i], rwkv_g_up[i], rwkv_k_k[i], rwkv_k_a[i], rwkv_r_k[i],
                        rwkv_ln_w[i], rwkv_ln_b[i], lb_layers[i], hgrn_norm[i], w_branch, w_o)
        ple = (p.reshape(depth, n, -1), ple_norm[i], ple_w_gate, ple_w_proj, final_norm)
        x = ffn_half_step(x, ffn2_norm[i], ffn2_w_gate, ffn2_w_up, ffn2_w_down, i, ple=ple, final=(i == depth - 1))
    return x.reshape(bsz, seq_len, d)
```

```python
import functools

import jax
import jax.numpy as jnp
from jax import lax
from jax.experimental import pallas as pl
from jax.experimental.pallas import tpu as pltpu

F32 = jnp.float32
BF16 = jnp.bfloat16

NORM_EPS = 1e-6
FFN_HALF = 0.5
MASK_VALUE = -1e30
HEAD_DIM = 64
ATTN_HEADS = 8
ATTN_BLOCK = 128
ATTN_SCALE = HEAD_DIM ** -0.5
assert ATTN_SCALE == 0.125
BRANCH_WIDTH = 512
RWKV_GN_EPS = 64e-5
RWKV_DECAY_SCALE = 0.6065306597126334
LOG2_E = 1.4426950408889634
HGRN_HEADS = 4
HGRN_DIM = 128

LANES = 128
VMEM_LIMIT_BYTES = 56 * 1024 * 1024

FFN_ROWS = 512
FFN_CHUNKS = 11
ROW_TILE = 512
WEIGHT_CHUNK = 256
ATTN_STEP_BLOCKS = 4
RWKV_CHUNK = 32
RWKV_STEP_CHUNKS = 4
RWKV_GROUP_HEADS = 4
RWKV_GROUP_LANES = RWKV_GROUP_HEADS * HEAD_DIM
HGRN_CHUNK = 16
HGRN_BLOCK = 256
HGRN_UNROLL = 8


def _params(semantics):
    return pltpu.CompilerParams(dimension_semantics=semantics, vmem_limit_bytes=VMEM_LIMIT_BYTES)


def _mm(a, b):
    return jnp.dot(a.astype(BF16), b.astype(BF16), preferred_element_type=F32)


def _mm_nt(a, b):
    return lax.dot_general(a.astype(BF16), b.astype(BF16), (((1,), (1,)), ((), ())), preferred_element_type=F32)


def _mm_tn(a, b):
    return lax.dot_general(a.astype(BF16), b.astype(BF16), (((0,), (0,)), ((), ())), preferred_element_type=F32)


def _split(x, pieces):
    out = []
    for _ in range(pieces - 1):
        p = x.astype(BF16)
        out.append(p)
        x = x - p.astype(F32)
    out.append(x.astype(BF16))
    return out


def _sel_mm(sel, x, pieces=3):
    sel = sel.astype(BF16)
    return sum(jnp.dot(sel, p, preferred_element_type=F32) for p in _split(x, pieces))


def _rms(x, g):
    ms = jnp.mean(x * x, axis=-1, keepdims=True)
    return x * lax.rsqrt(ms + NORM_EPS) * g


def _full(shape):
    return pl.BlockSpec(shape, lambda *_: (0,) * len(shape))


def _resident(shape):
    return pl.BlockSpec(shape, lambda *_: (0,) * len(shape), pipeline_mode=pl.Buffered(1))


def _layer_resident(layer, shape):
    return pl.BlockSpec((None,) + shape, lambda *_: (layer,) + (0,) * len(shape), pipeline_mode=pl.Buffered(1))


def _ffn_body(x_ref, g_ref, wg_ref, wu_ref, wd_ref, *rest, chunks, with_ple, final):
    i = pl.program_id(0)
    if with_ple:
        p_ref, pg_ref, pwg_ref, pwp_ref, gf_ref, o_ref, wg_s, wu_s, wd_s, pwg_s, pwp_s = rest
    else:
        o_ref, wg_s, wu_s, wd_s = rest

    @pl.when(i < chunks)
    def _():
        wg_s[i] = wg_ref[...].astype(BF16)
        wu_s[i] = wu_ref[...].astype(BF16)
        wd_s[i] = wd_ref[...].astype(BF16)
        if with_ple:
            @pl.when(i == 0)
            def _():
                pwg_s[...] = pwg_ref[...].astype(BF16)
                pwp_s[...] = pwp_ref[...].astype(BF16)

    @pl.when(i >= chunks)
    def _():
        x = x_ref[...]
        h = _rms(x, g_ref[...]).astype(BF16)
        acc = jnp.zeros(x.shape, F32)
        for k in range(chunks):
            gate = jnp.dot(h, wg_s[k], preferred_element_type=F32)
            up = jnp.dot(h, wu_s[k], preferred_element_type=F32)
            act = (gate * jax.nn.sigmoid(gate) * up).astype(BF16)
            acc = acc + jnp.dot(act, wd_s[k], preferred_element_type=F32)
        x = x + FFN_HALF * acc
        if with_ple:
            pgate = jax.nn.sigmoid(_mm(_rms(x, pg_ref[...]), pwg_s[...]))
            x = x + pgate * _mm(p_ref[...], pwp_s[...])
            if final:
                x = _rms(x, gf_ref[...])
        o_ref[...] = x


def ffn_half_step(x, g, wg, wu, wd, layer, ple=None, final=False):
    n, d = x.shape
    f = wg.shape[2]
    tm = min(FFN_ROWS, n)
    chunks = FFN_CHUNKS
    cf = f // chunks
    rows = lambda i: (jnp.maximum(i - chunks, 0), 0)
    col_chunk = lambda i: (layer, 0, jnp.minimum(i, chunks - 1))
    row_chunk = lambda i: (layer, jnp.minimum(i, chunks - 1), 0)
    row = lambda t: t.reshape(1, d)
    args = [x, row(g), wg, wu, wd]
    in_specs = [pl.BlockSpec((tm, d), rows), _full((1, d)), pl.BlockSpec((None, d, cf), col_chunk),
                pl.BlockSpec((None, d, cf), col_chunk), pl.BlockSpec((None, cf, d), row_chunk)]
    scratch = [pltpu.VMEM((chunks, d, cf), BF16), pltpu.VMEM((chunks, d, cf), BF16), pltpu.VMEM((chunks, cf, d), BF16)]
    if ple is not None:
        p, pg, pwg, pwp, gf = ple
        pw = p.shape[2]
        args += [p, row(pg), pwg, pwp, row(gf)]
        in_specs += [pl.BlockSpec((None, tm, pw), lambda i: (layer,) + rows(i)), _full((1, d)),
                     _layer_resident(layer, (d, d)), _layer_resident(layer, (pw, d)), _full((1, d))]
        scratch += [pltpu.VMEM((d, d), BF16), pltpu.VMEM((pw, d), BF16)]
    return pl.pallas_call(
        functools.partial(_ffn_body, chunks=chunks, with_ple=ple is not None, final=final),
        grid=(chunks + n // tm,),
        in_specs=in_specs,
        out_specs=pl.BlockSpec((tm, d), rows),
        out_shape=jax.ShapeDtypeStruct((n, d), F32),
        scratch_shapes=scratch,
        compiler_params=_params(("arbitrary",)),
        name="ffn_half_step",
    )(*args)


def _proj_body(x_ref, g_ref, w_ref, *rest, group_chunks):
    k = len(group_chunks)
    outs, scratch = rest[:k], rest[k:]
    total = sum(group_chunks)
    i = pl.program_id(0)
    first = 0
    for w_s, count in zip(scratch, group_chunks):
        @pl.when((i >= first) & (i < first + count))
        def _(w_s=w_s, first=first):
            w_s[i - first] = w_ref[...].astype(BF16)
        first += count

    @pl.when(i >= total)
    def _():
        h = _rms(x_ref[...], g_ref[...]).astype(BF16)
        for o_ref, w_s, count in zip(outs, scratch, group_chunks):
            for c in range(count):
                o_ref[:, c * WEIGHT_CHUNK:(c + 1) * WEIGHT_CHUNK] = jnp.dot(
                    h, w_s[c], preferred_element_type=F32).astype(o_ref.dtype)


def norm_project(x, g, w, layer, first_col, widths, out_dtypes):
    n, d = x.shape
    tm = min(ROW_TILE, n)
    ch = WEIGHT_CHUNK
    group_chunks = tuple(wd // ch for wd in widths)
    total = sum(group_chunks)
    rows = lambda i: (jnp.maximum(i - total, 0), 0)
    return pl.pallas_call(
        functools.partial(_proj_body, group_chunks=group_chunks),
        grid=(total + n // tm,),
        in_specs=[pl.BlockSpec((tm, d), rows), _full((1, d)),
                  pl.BlockSpec((None, d, ch), lambda i: (layer, 0, first_col // ch + jnp.minimum(i, total - 1)))],
        out_specs=[pl.BlockSpec((tm, wd), rows) for wd in widths],
        out_shape=[jax.ShapeDtypeStruct((n, wd), dt) for wd, dt in zip(widths, out_dtypes)],
        scratch_shapes=[pltpu.VMEM((c, d, ch), BF16) for c in group_chunks],
        compiler_params=_params(("arbitrary",)),
        name="norm_project",
    )(x, g.reshape(1, d), w)


def _attn_body(q_ref, kc_ref, kp_ref, vc_ref, vp_ref, sink_ref, o_ref, *, blocks_per_seq):
    blk = ATTN_BLOCK
    n_sub = q_ref.shape[0] // blk
    step = pl.program_id(0)
    lane = lax.broadcasted_iota(jnp.int32, (2 * blk, LANES), 1)
    low = lane < HEAD_DIM
    qi = lax.broadcasted_iota(jnp.int32, (blk, 2 * blk), 0)
    kj = lax.broadcasted_iota(jnp.int32, (blk, 2 * blk), 1)
    rel = blk + qi - kj
    band = (rel >= 0) & (rel < blk)

    def placed(x, x_sw, kv_head, low_half):
        src = x if (kv_head == 0) == low_half else x_sw
        return jnp.where(low if low_half else jnp.logical_not(low), src, 0.0).astype(BF16)

    probs = [(j, kv, half) for j in range(n_sub) for kv in range(2) for half in range(2)]
    masks, k_ext, v_ext, qs = [], {}, {}, {}
    for j in range(n_sub):
        rows = slice(j * blk, (j + 1) * blk)
        prev = slice((j - 1) * blk, j * blk)
        kb = jnp.concatenate([kp_ref[...] if j == 0 else kc_ref[prev, :], kc_ref[rows, :]], axis=0).astype(F32)
        vb = jnp.concatenate([vp_ref[...] if j == 0 else vc_ref[prev, :], vc_ref[rows, :]], axis=0).astype(F32)
        kb_sw = pltpu.roll(kb, HEAD_DIM, 1)
        vb_sw = pltpu.roll(vb, HEAD_DIM, 1)
        first = ((step * n_sub + j) % blocks_per_seq == 0).astype(jnp.int32)
        masks.append(band & (kj >= first * blk))
        for kv in range(2):
            qs[j, kv] = ATTN_SCALE * jnp.concatenate([q_ref[rows, (2 * kv) * LANES:(2 * kv + 1) * LANES],
                                                      q_ref[rows, (2 * kv + 1) * LANES:(2 * kv + 2) * LANES]], axis=0)
            for half in range(2):
                k_ext[j, kv, half] = placed(kb, kb_sw, kv, half == 0)
                v_ext[j, kv, half] = placed(vb, vb_sw, kv, half == 0)
    scores = [_mm_nt(qs[j, kv], k_ext[j, kv, half]) for j, kv, half in probs]
    sinks = [jnp.max(sink_ref[h:h + 1, :], axis=-1, keepdims=True) for h in range(ATTN_HEADS)]
    weights = []
    for i, (j, kv, half) in enumerate(probs):
        parts = []
        for pair in range(2):
            sink = sinks[4 * kv + 2 * pair + half]
            s = jnp.where(masks[j], scores[i][pair * blk:(pair + 1) * blk], MASK_VALUE)
            m = jnp.maximum(jnp.max(s, axis=-1, keepdims=True), sink)
            e = jnp.exp(s - m)
            denom = jnp.sum(e, axis=-1, keepdims=True) + jnp.exp(sink - m)
            parts.append((e * (1.0 / denom)).astype(BF16))
        weights.append(jnp.concatenate(parts, axis=0))
    outs = [_mm(weights[i], v_ext[p]) for i, p in enumerate(probs)]
    for j in range(n_sub):
        for kv in range(2):
            i = (j * 2 + kv) * 2
            out = outs[i] + outs[i + 1]
            rows = slice(j * blk, (j + 1) * blk)
            o_ref[rows, (2 * kv) * LANES:(2 * kv + 1) * LANES] = out[:blk].astype(o_ref.dtype)
            o_ref[rows, (2 * kv + 1) * LANES:(2 * kv + 2) * LANES] = out[blk:].astype(o_ref.dtype)


def sliding_window_attention(z_attn, sinks, seq_len):
    n = z_attn.shape[0]
    blk = ATTN_BLOCK
    span = min(ATTN_STEP_BLOCKS * blk, seq_len)
    sub = span // blk
    qw = ATTN_HEADS * HEAD_DIM
    kcol = qw // LANES
    vcol = kcol + 1
    sink_rows = jnp.broadcast_to(sinks.astype(F32)[:, None], (ATTN_HEADS, LANES))
    prev = lambda i: jnp.maximum(i * sub - 1, 0)
    return pl.pallas_call(
        functools.partial(_attn_body, blocks_per_seq=seq_len // blk),
        grid=(n // span,),
        in_specs=[
            pl.BlockSpec((span, qw), lambda i: (i, 0)),
            pl.BlockSpec((span, LANES), lambda i: (i, kcol)),
            pl.BlockSpec((blk, LANES), lambda i: (prev(i), kcol)),
            pl.BlockSpec((span, LANES), lambda i: (i, vcol)),
            pl.BlockSpec((blk, LANES), lambda i: (prev(i), vcol)),
            _full((ATTN_HEADS, LANES)),
        ],
        out_specs=pl.BlockSpec((span, qw), lambda i: (i, 0)),
        out_shape=jax.ShapeDtypeStruct((n, qw), BF16),
        compiler_params=_params(("parallel",)),
        name="sliding_window_attention",
    )(z_attn, z_attn, z_attn, z_attn, z_attn, sink_rows)


def _head_ones(width, head):
    idx = jnp.arange(width) // head
    return (idx[:, None] == idx[None, :]).astype(BF16)


def _chunk_selectors(rows, chunk):
    r = jnp.arange(rows)
    prefix = (r[:, None] // chunk == r[None, :] // chunk) & (r[None, :] <= r[:, None])
    member = jnp.arange(rows // chunk)[:, None] == r[None, :] // chunk
    return prefix.astype(BF16), member.astype(BF16)


def _rwkv_pre_body(z_ref, zp_ref, mu_ref, w0_ref, wup_ref, a0_ref, aup_ref, gup_ref, kk_ref, ka_ref, rk_ref,
                   ones_ref, prefix_ref, member_ref,
                   ad_o, rd_o, bd_o, kd_o, be_o, ke_o, v_o, gam_o, bonus_o, g_o, *, tiles_per_seq):
    bw = BRANCH_WIDTH
    i = pl.program_id(0)
    z = z_ref[...]
    keep = (i % tiles_per_seq != 0).astype(F32)
    prev_row = zp_ref[7:8, :] * keep
    row = lax.broadcasted_iota(jnp.int32, z.shape, 0)
    z_prev = jnp.where(row == 0, prev_row, pltpu.roll(z, 1, 0))
    z = z + mu_ref[...] * (z_prev - z)
    r = z[:, 0:bw]
    k = z[:, bw:2 * bw]
    v = z[:, 2 * bw:3 * bw]
    x_wa = z[:, 3 * bw:3 * bw + LANES]
    x_g = z[:, 3 * bw + LANES:3 * bw + 2 * LANES]
    d = w0_ref[...] + _mm(jnp.tanh(x_wa), wup_ref[...])
    lw = -RWKV_DECAY_SCALE * jax.nn.sigmoid(d)
    a_gate = jax.nn.sigmoid(a0_ref[...] + _mm(x_wa, aup_ref[...]))
    g = _mm(jax.nn.sigmoid(x_g), gup_ref[...])
    kk = k * kk_ref[...]
    ones = ones_ref[...]
    kk = kk * lax.rsqrt(jnp.maximum(_mm(kk * kk, ones), 1e-24))
    k = k * (1.0 + (a_gate - 1.0) * ka_ref[...])
    chunk = RWKV_CHUNK
    tm = z.shape[0]
    cs = _sel_mm(prefix_ref[...], lw, pieces=2)
    tot_c = _sel_mm(member_ref[...], lw)
    gam_o[:, 0, :] = jnp.exp(tot_c)
    tot = jnp.broadcast_to(tot_c[:, None, :], (tm // chunk, chunk, bw)).reshape(tm, bw)
    dec_out = jnp.exp(-cs)
    dec_end = jnp.exp(tot - cs)
    b = kk * a_gate
    ad_o[...] = (-kk * jnp.exp(cs - lw)).astype(ad_o.dtype)
    rd_o[...] = (r * jnp.exp(cs)).astype(rd_o.dtype)
    bd_o[...] = (b * dec_out).astype(bd_o.dtype)
    kd_o[...] = (k * dec_out).astype(kd_o.dtype)
    be_o[...] = (b * dec_end).astype(be_o.dtype)
    ke_o[...] = (k * dec_end).astype(ke_o.dtype)
    v_o[...] = v.astype(v_o.dtype)
    bonus_o[...] = _mm(r * k * rk_ref[...], ones) * v
    g_o[...] = g


def rwkv_prepare(z, mu, w0, w_up, a0, a_up, g_up, k_k, k_a, r_k, seq_len):
    n, zw = z.shape
    bw = BRANCH_WIDTH
    tm = min(ROW_TILE, seq_len)
    half = w_up.shape[0]
    zeros = jnp.zeros((LANES - half, bw), BF16)
    wup_ext = jnp.concatenate([w_up.astype(BF16), zeros], axis=0)
    aup_ext = jnp.concatenate([zeros, a_up.astype(BF16)], axis=0)
    row = lambda t: t.reshape(1, -1).astype(F32)
    tile = pl.BlockSpec((tm, bw), lambda i: (i, 0))
    cpt = tm // RWKV_CHUNK
    return pl.pallas_call(
        functools.partial(_rwkv_pre_body, tiles_per_seq=seq_len // tm),
        grid=(n // tm,),
        in_specs=[
            pl.BlockSpec((tm, zw), lambda i: (i, 0)),
            pl.BlockSpec((8, zw), lambda i: (jnp.maximum(i * (tm // 8) - 1, 0), 0)),
            _full((1, zw)), _full((1, bw)), _full((LANES, bw)), _full((1, bw)), _full((LANES, bw)),
            _full((LANES, bw)), _full((1, bw)), _full((1, bw)), _full((1, bw)),
            _resident((bw, bw)), _resident((tm, tm)), _resident((cpt, tm)),
        ],
        out_specs=[tile] * 7 + [pl.BlockSpec((cpt, 1, bw), lambda i: (i, 0, 0)), tile, tile],
        out_shape=[jax.ShapeDtypeStruct((n, bw), BF16)] * 7
        + [jax.ShapeDtypeStruct((n // RWKV_CHUNK, 1, bw), F32)] + [jax.ShapeDtypeStruct((n, bw), F32)] * 2,
        compiler_params=_params(("parallel",)),
        name="rwkv_prepare",
    )(z, z, row(mu), row(w0), wup_ext, row(a0), aup_ext, g_up.astype(BF16), row(k_k), row(k_a), row(r_k),
      _head_ones(bw, HEAD_DIM), *_chunk_selectors(tm, RWKV_CHUNK))


def _rwkv_scan_body(ad_ref, rd_ref, bd_ref, kd_ref, be_ref, ke_ref, v_ref, gam_ref, y_ref, s_ref):
    c = pl.program_id(0)

    @pl.when(c == 0)
    def _():
        s_ref[...] = jnp.zeros_like(s_ref)

    batch = v_ref.shape[0]
    chunk = RWKV_CHUNK
    n_chunks = v_ref.shape[1] // chunk
    gh, gl = RWKV_GROUP_HEADS, RWKV_GROUP_LANES
    rows = gh * chunk
    row_head = lax.broadcasted_iota(jnp.int32, (rows, gl), 0) // chunk
    lane_head = lax.broadcasted_iota(jnp.int32, (rows, gl), 1) // HEAD_DIM
    own = row_head == lane_head
    ii = lax.broadcasted_iota(jnp.int32, (rows, rows), 0)
    jj = lax.broadcasted_iota(jnp.int32, (rows, rows), 1)
    strict = jj < ii
    incl = jj <= ii
    eye = (ii == jj).astype(F32)

    def stack(x):
        return jnp.where(own, jnp.concatenate([x] * gh, axis=0), 0.0).astype(BF16)

    chains = [(bi, g) for bi in range(batch) for g in range(BRANCH_WIDTH // gl)]
    probs = [(c, bi, g) for c in range(n_chunks) for bi, g in chains]
    npb = range(len(probs))
    operand_refs = (ad_ref, rd_ref, bd_ref, kd_ref, be_ref, ke_ref, v_ref)
    ar, bk, v_s, be_ke, tot = [], [], [], [], []
    for c, bi, g in probs:
        rs = slice(c * chunk, (c + 1) * chunk)
        sl = slice(g * gl, (g + 1) * gl)
        ad, rd, bd, kd, be, ke, v = (ref[bi, rs, sl] for ref in operand_refs)
        ar.append(jnp.concatenate([stack(ad), stack(rd)], axis=0))
        bk.append(jnp.concatenate([stack(bd), stack(kd)], axis=0))
        v_s.append(stack(v))
        be_ke.append(jnp.concatenate([stack(be), stack(ke)], axis=0))
        tot.append(gam_ref[bi, c, :, sl])
    gram = [_mm_nt(ar[i], bk[i]) for i in npb]
    nmat = [jnp.where(strict, gram[i][:rows, :rows], 0.0).astype(BF16) for i in npb]
    m_ak = [jnp.where(strict, gram[i][:rows, rows:], 0.0).astype(BF16) for i in npb]
    m_rbk = [jnp.where(jnp.concatenate([incl, incl], axis=1), gram[i][rows:, :], 0.0).astype(BF16) for i in npb]
    tinv = [eye + nmat[i].astype(F32) for i in npb]
    for _ in range(chunk.bit_length() - 2):
        nmat = [_mm(nmat[i], nmat[i]).astype(BF16) for i in npb]
        tinv = [tinv[i] + _mm(tinv[i], nmat[i]) for i in npb]
    tinv = [tinv[i].astype(BF16) for i in npb]
    x0 = [_mm(m_ak[i], v_s[i]) for i in npb]

    sts = [s_ref[bi, g] for bi, g in chains]
    ncn = len(chains)
    for c in range(n_chunks):
        ids = [c * ncn + j for j in range(ncn)]
        ars = [_mm_nt(ar[i], sts[j]) for j, i in enumerate(ids)]
        sa = [_mm(tinv[i], ars[j][:rows] + x0[i]).astype(BF16) for j, i in enumerate(ids)]
        sv = [jnp.concatenate([sa[j], v_s[i]], axis=0) for j, i in enumerate(ids)]
        y_s = [ars[j][rows:] + _mm(m_rbk[i], sv[j]) for j, i in enumerate(ids)]
        upd = [_mm_tn(sv[j], be_ke[i]) for j, i in enumerate(ids)]
        sts = [sts[j] * tot[i] + upd[j] for j, i in enumerate(ids)]
        for j, (bi, g) in enumerate(chains):
            y = y_s[j][0:chunk]
            for h in range(1, gh):
                y = y + y_s[j][h * chunk:(h + 1) * chunk]
            y_ref[bi, c * chunk:(c + 1) * chunk, g * gl:(g + 1) * gl] = y
    for j, (bi, g) in enumerate(chains):
        s_ref[bi, g] = sts[j]


def rwkv_scan(ad, rd, bd, kd, be, ke, v, gam):
    bsz, seq, bw = v.shape
    steps = min(RWKV_STEP_CHUNKS, seq // RWKV_CHUNK)
    span = steps * RWKV_CHUNK
    blk = pl.BlockSpec((bsz, span, bw), lambda c: (0, c, 0))
    groups = bw // RWKV_GROUP_LANES
    return pl.pallas_call(
        _rwkv_scan_body,
        grid=(seq // span,),
        in_specs=[blk] * 7 + [pl.BlockSpec((bsz, steps, 1, bw), lambda c: (0, c, 0, 0))],
        out_specs=blk,
        out_shape=jax.ShapeDtypeStruct((bsz, seq, bw), F32),
        scratch_shapes=[pltpu.VMEM((bsz, groups, RWKV_GROUP_LANES, RWKV_GROUP_LANES), F32)],
        compiler_params=_params(("arbitrary",)),
        name="rwkv_scan",
    )(ad, rd, bd, kd, be, ke, v, gam)


def _hgrn_body(z_ref, lb_ref, ng_ref, o_ref, s_ref, c_ref):
    c = pl.program_id(0)

    @pl.when(c == 0)
    def _():
        s_ref[...] = jnp.zeros_like(s_ref)

    batch, rows, _ = z_ref.shape
    bw = BRANCH_WIDTH
    hd = HGRN_DIM
    ch = HGRN_CHUNK
    half = ch // 2
    upto = (lax.broadcasted_iota(jnp.int32, (ch, ch), 1) <= lax.broadcasted_iota(jnp.int32, (ch, ch), 0))
    row8 = lax.broadcasted_iota(jnp.int32, (half, hd), 0)
    lane8 = lax.broadcasted_iota(jnp.int32, (half, hd), 1)
    chains = [(bi, h) for bi in range(batch) for h in range(HGRN_HEADS)]

    def gates(rs, slot):
        lb = lb_ref[...]
        q, qe, ke, gam, og, b2, val = [], [], [], [], [], [], []
        for bi in range(batch):
            zq = z_ref[bi, rs, 0:bw]
            zf = z_ref[bi, rs, bw:2 * bw]
            zo = z_ref[bi, rs, 3 * bw:4 * bw]
            forget = lb + (1.0 - lb) * jax.nn.sigmoid(zf)
            key = 1.0 - forget
            b = _sel_mm(upto, jnp.log(forget))
            tot = b[ch - 1:ch, :]
            qb = zq * jax.nn.sigmoid(zq)
            q.append(qb)
            qe.append((qb * jnp.exp(b)).astype(BF16))
            ke.append((key * jnp.exp(tot - b)).astype(BF16))
            gam.append(jnp.exp(tot))
            og.append(zo * jax.nn.sigmoid(zo))
            val.append(z_ref[bi, rs, 2 * bw:3 * bw].astype(BF16))
            b2.append(b * LOG2_E)
            c_ref[slot, bi] = (b - jnp.log(key)) * LOG2_E
        return q, qe, ke, gam, og, b2, val

    def chunk_weights(q, b2, slot):
        attn = []
        for bi, h in chains:
            hs = slice(h * hd, (h + 1) * hd)
            b_lo = b2[bi][0:half, hs]
            b_hi = b2[bi][half:ch, hs]
            q_lo = q[bi][0:half, hs]
            q_hi = q[bi][half:ch, hs]
            a_lo = jnp.zeros((half, hd), F32)
            a_hi = jnp.zeros((half, hd), F32)
            for s in range(ch):
                c_s = c_ref[slot, bi, s:s + 1, hs]
                if s < half:
                    w = q_lo * jnp.exp2(jnp.where(row8 >= s, b_lo - c_s, MASK_VALUE))
                    a_lo = jnp.where(lane8 == s, jnp.sum(w, axis=-1, keepdims=True), a_lo)
                    w = q_hi * jnp.exp2(b_hi - c_s)
                else:
                    w = q_hi * jnp.exp2(jnp.where(row8 >= s - half, b_hi - c_s, MASK_VALUE))
                a_hi = jnp.where(lane8 == s, jnp.sum(w, axis=-1, keepdims=True), a_hi)
            attn.append(jnp.concatenate([a_lo, a_hi], axis=0)[:, 0:ch].astype(BF16))
        return attn

    def chunk_group(jj, carry):
        slots = range(HGRN_UNROLL)
        rss = [pl.ds(pl.multiple_of((jj * HGRN_UNROLL + u) * ch, ch), ch) for u in slots]
        gated = [gates(rss[u], u) for u in slots]
        attn = [chunk_weights(gated[u][0], gated[u][5], u) for u in slots]
        sts = [s_ref[bi, h] for bi, h in chains]
        outs = []
        for u in slots:
            _, qe, ke, gam, og, _, val = gated[u]
            hsl = [slice(h * hd, (h + 1) * hd) for _, h in chains]
            inter = [_mm_nt(qe[bi][:, hsl[i]], sts[i]) for i, (bi, _) in enumerate(chains)]
            upd = [_mm_tn(val[bi][:, hsl[i]], ke[bi][:, hsl[i]]) for i, (bi, _) in enumerate(chains)]
            intra = [_mm(attn[u][i], val[bi][:, hsl[i]]) for i, (bi, _) in enumerate(chains)]
            outs.append([_rms(inter[i] + intra[i], ng_ref[...]) * og[bi][:, hsl[i]]
                         for i, (bi, _) in enumerate(chains)])
            sts = [sts[i] * gam[bi][:, hsl[i]] + upd[i] for i, (bi, _) in enumerate(chains)]
        for u in slots:
            for i, (bi, h) in enumerate(chains):
                o_ref[bi, rss[u], h * hd:(h + 1) * hd] = outs[u][i].astype(o_ref.dtype)
        for i, (bi, h) in enumerate(chains):
            s_ref[bi, h] = sts[i]
        return carry

    lax.fori_loop(0, rows // (ch * HGRN_UNROLL), chunk_group, 0)


def hgrn_mix(z, lb, norm_g):
    bsz, seq, zw = z.shape
    bw = BRANCH_WIDTH
    rows = min(HGRN_BLOCK, seq)
    blk = pltpu.VMEM((HGRN_UNROLL, bsz, HGRN_CHUNK, bw), F32)
    return pl.pallas_call(
        _hgrn_body,
        grid=(seq // rows,),
        in_specs=[pl.BlockSpec((bsz, rows, zw), lambda c: (0, c, 0)), _full((1, bw)), _full((1, HGRN_DIM))],
        out_specs=pl.BlockSpec((bsz, rows, bw), lambda c: (0, c, 0)),
        out_shape=jax.ShapeDtypeStruct((bsz, seq, bw), BF16),
        scratch_shapes=[pltpu.VMEM((bsz, HGRN_HEADS, HGRN_DIM, HGRN_DIM), F32), blk],
        compiler_params=_params(("arbitrary",)),
        name="hgrn_mix",
    )(z, lb.reshape(1, bw).astype(F32), norm_g.reshape(1, HGRN_DIM).astype(F32))


def _merge_body(x_ref, g_ref, ya_ref, y_ref, bonus_ref, gg_ref, lnw_ref, lnb_ref, ones_ref, yc_ref, wgate_ref, wb_ref,
                wo_ref, o_ref, wgate_s, wb_s, wo_s, *, chunks):
    i = pl.program_id(0)
    n_branch, wo_chunks = wb_s.shape[0], wo_s.shape[0]

    @pl.when(i < chunks)
    def _():
        wgate_s[i] = wgate_ref[...].astype(BF16)

        @pl.when(i < n_branch)
        def _():
            wb_s[i] = wb_ref[0].astype(BF16)

        @pl.when(i < wo_chunks)
        def _():
            wo_s[i] = wo_ref[...].astype(BF16)

    @pl.when(i >= chunks)
    def _():
        x = x_ref[...]
        d = x.shape[1]
        per = d // WEIGHT_CHUNK
        h = _rms(x, g_ref[...]).astype(BF16)
        ones = ones_ref[...]
        y = y_ref[...]
        yc = y - _mm(y, ones) * (1.0 / HEAD_DIM)
        var = _mm(yc * yc, ones) * (1.0 / HEAD_DIM)
        yn = yc * lax.rsqrt(var + RWKV_GN_EPS) * lnw_ref[...] + lnb_ref[...]
        yb = ((yn + bonus_ref[...]) * gg_ref[...]).astype(BF16)
        merged = jnp.zeros(x.shape, F32)
        for n, branch in enumerate((ya_ref[...], yb, yc_ref[...])):
            gate = jnp.concatenate([jnp.dot(h, wgate_s[n * per + c], preferred_element_type=F32)
                                    for c in range(per)], axis=1)
            merged = merged + jax.nn.sigmoid(gate) * jnp.dot(branch, wb_s[n], preferred_element_type=F32)
        merged = merged.astype(BF16)
        out = x
        for c in range(wo_chunks):
            rows = slice(c * WEIGHT_CHUNK, (c + 1) * WEIGHT_CHUNK)
            out = out + jnp.dot(merged[:, rows], wo_s[c], preferred_element_type=F32)
        o_ref[...] = out


def merge_branches(x, g, ya, y, bonus, gg, ln_w, ln_b, yc, w_in, gate_col, w_branch, w_o, layer):
    n, d = x.shape
    bw = ya.shape[1]
    nb = w_branch.shape[1]
    tm = min(ROW_TILE, n)
    ch = WEIGHT_CHUNK
    chunks = nb * d // ch
    rows = lambda i: (jnp.maximum(i - chunks, 0), 0)
    ytile = pl.BlockSpec((tm, bw), rows)
    return pl.pallas_call(
        functools.partial(_merge_body, chunks=chunks),
        grid=(chunks + n // tm,),
        in_specs=[pl.BlockSpec((tm, d), rows), _full((1, d)), ytile, ytile, ytile, ytile, _full((1, bw)), _full((1, bw)),
                  _resident((bw, bw)), ytile,
                  pl.BlockSpec((None, d, ch), lambda i: (layer, 0, gate_col // ch + jnp.minimum(i, chunks - 1))),
                  pl.BlockSpec((None, 1, bw, d), lambda i: (layer, jnp.minimum(i, nb - 1), 0, 0)),
                  pl.BlockSpec((None, ch, d), lambda i: (layer, jnp.minimum(i, d // ch - 1), 0))],
        out_specs=pl.BlockSpec((tm, d), rows),
        out_shape=jax.ShapeDtypeStruct((n, d), F32),
        scratch_shapes=[pltpu.VMEM((chunks, d, ch), BF16), pltpu.VMEM((nb, bw, d), BF16),
                        pltpu.VMEM((d // ch, ch, d), BF16)],
        compiler_params=_params(("arbitrary",)),
        name="merge_branches",
    )(x, g.reshape(1, d), ya, y, bonus, gg, ln_w.reshape(1, bw).astype(F32), ln_b.reshape(1, bw).astype(F32),
      _head_ones(bw, HEAD_DIM), yc, w_in, w_branch, w_o)


def token_mixer(x, seq_len, layer, mix_norm, w_in, attn_sinks, rwkv_mu, rwkv_w0, rwkv_w_up, rwkv_a0, rwkv_a_up, rwkv_g_up,
                rwkv_k_k, rwkv_k_a, rwkv_r_k, rwkv_ln_w, rwkv_ln_b, lb, hgrn_norm, w_branch, w_o):
    n, d = x.shape
    bsz = n // seq_len
    bw = BRANCH_WIDTH
    attn_w = (ATTN_HEADS + 2 * (ATTN_HEADS // 4)) * HEAD_DIM
    rwkv_w = rwkv_mu.shape[0]
    hgrn_w = 4 * bw
    o3 = attn_w + rwkv_w + hgrn_w
    z_attn, z_rwkv, z_hgrn = norm_project(x, mix_norm, w_in, layer, 0, (attn_w, rwkv_w, hgrn_w), (BF16, F32, F32))

    y_a = sliding_window_attention(z_attn, attn_sinks, seq_len)

    *operands, gam, bonus, g = rwkv_prepare(z_rwkv, rwkv_mu, rwkv_w0, rwkv_w_up, rwkv_a0, rwkv_a_up, rwkv_g_up,
                                            rwkv_k_k, rwkv_k_a, rwkv_r_k, seq_len)
    y = rwkv_scan(*(t.reshape(bsz, seq_len, bw) for t in operands),
                  gam.reshape(bsz, seq_len // RWKV_CHUNK, 1, bw)).reshape(n, bw)
    y_c = hgrn_mix(z_hgrn.reshape(bsz, seq_len, hgrn_w), lb, hgrn_norm).reshape(n, bw)

    return merge_branches(x, mix_norm, y_a, y, bonus, g, rwkv_ln_w, rwkv_ln_b, y_c, w_in, o3, w_branch, w_o, layer)


def kernel(x, p, ffn1_norm, ffn1_w_gate, ffn1_w_up, ffn1_w_down, mix_norm, w_in, attn_sinks, rwkv_mu, rwkv_w0,
           rwkv_w_up, rwkv_a0, rwkv_a_up, rwkv_g_up, rwkv_k_k, rwkv_k_a, rwkv_r_k, rwkv_ln_w, rwkv_ln_b, hgrn_lb,
           hgrn_norm, w_branch, w_o, ffn2_norm, ffn2_w_gate, ffn2_w_up, ffn2_w_down, ple_norm, ple_w_gate,
           ple_w_proj, final_norm):
    bsz, seq_len, d = x.shape
    depth = p.shape[0]
    n = bsz * seq_len
    lb_all = jax.nn.softmax(hgrn_lb.astype(F32), axis=0)
    lb_layers = jnp.clip(jnp.cumsum(lb_all, axis=0) - lb_all[0:1], 0.0, 1.0 - 1e-6)
    x = x.reshape(n, d)
    for i in range(depth):
        x = ffn_half_step(x, ffn1_norm[i], ffn1_w_gate, ffn1_w_up, ffn1_w_down, i)
        x = token_mixer(x, seq_len, i, mix_norm[i], w_in, attn_sinks[i], rwkv_mu[i], rwkv_w0[i], rwkv_w_up[i],
                        rwkv_a0[i], rwkv_a_up[i], rwkv_g_up[i], rwkv_k_k[i], rwkv_k_a[i], rwkv_r_k[i],
                        rwkv_ln_w[i], rwkv_ln_b[i], lb_layers[i], hgrn_norm[i], w_branch, w_o)
        ple = (p.reshape(depth, n, -1), ple_norm[i], ple_w_gate, ple_w_proj, final_norm)
        x = ffn_half_step(x, ffn2_norm[i], ffn2_w_gate, ffn2_w_up, ffn2_w_down, i, ple=ple, final=(i == depth - 1))
    return x.reshape(bsz, seq_len, d)
```

```python
import functools

import jax
import jax.numpy as jnp
from jax import lax
from jax.experimental import pallas as pl
from jax.experimental.pallas import tpu as pltpu

F32 = jnp.float32
BF16 = jnp.bfloat16

NORM_EPS = 1e-6
FFN_HALF = 0.5
MASK_VALUE = -1e30
HEAD_DIM = 64
ATTN_HEADS = 8
ATTN_BLOCK = 128
ATTN_SCALE = HEAD_DIM ** -0.5
assert ATTN_SCALE == 0.125
BRANCH_WIDTH = 512
RWKV_GN_EPS = 64e-5
RWKV_DECAY_SCALE = 0.6065306597126334
LOG2_E = 1.4426950408889634
HGRN_HEADS = 4
HGRN_DIM = 128

LANES = 128
VMEM_LIMIT_BYTES = 56 * 1024 * 1024

FFN_ROWS = 512
FFN_CHUNKS = 11
ROW_TILE = 512
WEIGHT_CHUNK = 256
ATTN_STEP_BLOCKS = 4
RWKV_CHUNK = 32
RWKV_STEP_CHUNKS = 8
RWKV_GROUP_HEADS = 4
RWKV_GROUP_LANES = RWKV_GROUP_HEADS * HEAD_DIM
HGRN_CHUNK = 16
HGRN_BLOCK = 256
HGRN_UNROLL = 8


def _params(semantics):
    return pltpu.CompilerParams(dimension_semantics=semantics, vmem_limit_bytes=VMEM_LIMIT_BYTES)


def _mm(a, b):
    return jnp.dot(a.astype(BF16), b.astype(BF16), preferred_element_type=F32)


def _mm_nt(a, b):
    return lax.dot_general(a.astype(BF16), b.astype(BF16), (((1,), (1,)), ((), ())), preferred_element_type=F32)


def _mm_tn(a, b):
    return lax.dot_general(a.astype(BF16), b.astype(BF16), (((0,), (0,)), ((), ())), preferred_element_type=F32)


def _split(x, pieces):
    out = []
    for _ in range(pieces - 1):
        p = x.astype(BF16)
        out.append(p)
        x = x - p.astype(F32)
    out.append(x.astype(BF16))
    return out


def _sel_mm(sel, x, pieces=3):
    sel = sel.astype(BF16)
    return sum(jnp.dot(sel, p, preferred_element_type=F32) for p in _split(x, pieces))


def _rms(x, g):
    ms = jnp.mean(x * x, axis=-1, keepdims=True)
    return x * lax.rsqrt(ms + NORM_EPS) * g


def _full(shape):
    return pl.BlockSpec(shape, lambda *_: (0,) * len(shape))


def _resident(shape):
    return pl.BlockSpec(shape, lambda *_: (0,) * len(shape), pipeline_mode=pl.Buffered(1))


def _layer_resident(layer, shape):
    return pl.BlockSpec((None,) + shape, lambda *_: (layer,) + (0,) * len(shape), pipeline_mode=pl.Buffered(1))


def _ffn_body(x_ref, g_ref, wg_ref, wu_ref, wd_ref, *rest, chunks, with_ple, final):
    i = pl.program_id(0)
    if with_ple:
        p_ref, pg_ref, pwg_ref, pwp_ref, gf_ref, o_ref, wg_s, wu_s, wd_s, pwg_s, pwp_s = rest
    else:
        o_ref, wg_s, wu_s, wd_s = rest

    @pl.when(i < chunks)
    def _():
        wg_s[i] = wg_ref[...].astype(BF16)
        wu_s[i] = wu_ref[...].astype(BF16)
        wd_s[i] = wd_ref[...].astype(BF16)
        if with_ple:
            @pl.when(i == 0)
            def _():
                pwg_s[...] = pwg_ref[...].astype(BF16)
                pwp_s[...] = pwp_ref[...].astype(BF16)

    @pl.when(i >= chunks)
    def _():
        x = x_ref[...]
        h = _rms(x, g_ref[...]).astype(BF16)
        acc = jnp.zeros(x.shape, F32)
        for k in range(chunks):
            gate = jnp.dot(h, wg_s[k], preferred_element_type=F32)
            up = jnp.dot(h, wu_s[k], preferred_element_type=F32)
            act = (gate * jax.nn.sigmoid(gate) * up).astype(BF16)
            acc = acc + jnp.dot(act, wd_s[k], preferred_element_type=F32)
        x = x + FFN_HALF * acc
        if with_ple:
            pgate = jax.nn.sigmoid(_mm(_rms(x, pg_ref[...]), pwg_s[...]))
            x = x + pgate * _mm(p_ref[...], pwp_s[...])
            if final:
                x = _rms(x, gf_ref[...])
        o_ref[...] = x


def ffn_half_step(x, g, wg, wu, wd, layer, ple=None, final=False):
    n, d = x.shape
    f = wg.shape[2]
    tm = min(FFN_ROWS, n)
    chunks = FFN_CHUNKS
    cf = f // chunks
    rows = lambda i: (jnp.maximum(i - chunks, 0), 0)
    col_chunk = lambda i: (layer, 0, jnp.minimum(i, chunks - 1))
    row_chunk = lambda i: (layer, jnp.minimum(i, chunks - 1), 0)
    row = lambda t: t.reshape(1, d)
    args = [x, row(g), wg, wu, wd]
    in_specs = [pl.BlockSpec((tm, d), rows), _full((1, d)), pl.BlockSpec((None, d, cf), col_chunk),
                pl.BlockSpec((None, d, cf), col_chunk), pl.BlockSpec((None, cf, d), row_chunk)]
    scratch = [pltpu.VMEM((chunks, d, cf), BF16), pltpu.VMEM((chunks, d, cf), BF16), pltpu.VMEM((chunks, cf, d), BF16)]
    if ple is not None:
        p, pg, pwg, pwp, gf = ple
        pw = p.shape[2]
        args += [p, row(pg), pwg, pwp, row(gf)]
        in_specs += [pl.BlockSpec((None, tm, pw), lambda i: (layer,) + rows(i)), _full((1, d)),
                     _layer_resident(layer, (d, d)), _layer_resident(layer, (pw, d)), _full((1, d))]
        scratch += [pltpu.VMEM((d, d), BF16), pltpu.VMEM((pw, d), BF16)]
    return pl.pallas_call(
        functools.partial(_ffn_body, chunks=chunks, with_ple=ple is not None, final=final),
        grid=(chunks + n // tm,),
        in_specs=in_specs,
        out_specs=pl.BlockSpec((tm, d), rows),
        out_shape=jax.ShapeDtypeStruct((n, d), F32),
        scratch_shapes=scratch,
        compiler_params=_params(("arbitrary",)),
        name="ffn_half_step",
    )(*args)


def _proj_body(x_ref, g_ref, w_ref, *rest, group_chunks):
    k = len(group_chunks)
    outs, scratch = rest[:k], rest[k:]
    total = sum(group_chunks)
    i = pl.program_id(0)
    first = 0
    for w_s, count in zip(scratch, group_chunks):
        @pl.when((i >= first) & (i < first + count))
        def _(w_s=w_s, first=first):
            w_s[i - first] = w_ref[...].astype(BF16)
        first += count

    @pl.when(i >= total)
    def _():
        h = _rms(x_ref[...], g_ref[...]).astype(BF16)
        for o_ref, w_s, count in zip(outs, scratch, group_chunks):
            for c in range(count):
                o_ref[:, c * WEIGHT_CHUNK:(c + 1) * WEIGHT_CHUNK] = jnp.dot(
                    h, w_s[c], preferred_element_type=F32).astype(o_ref.dtype)


def norm_project(x, g, w, layer, first_col, widths, out_dtypes):
    n, d = x.shape
    tm = min(ROW_TILE, n)
    ch = WEIGHT_CHUNK
    group_chunks = tuple(wd // ch for wd in widths)
    total = sum(group_chunks)
    rows = lambda i: (jnp.maximum(i - total, 0), 0)
    return pl.pallas_call(
        functools.partial(_proj_body, group_chunks=group_chunks),
        grid=(total + n // tm,),
        in_specs=[pl.BlockSpec((tm, d), rows), _full((1, d)),
                  pl.BlockSpec((None, d, ch), lambda i: (layer, 0, first_col // ch + jnp.minimum(i, total - 1)))],
        out_specs=[pl.BlockSpec((tm, wd), rows) for wd in widths],
        out_shape=[jax.ShapeDtypeStruct((n, wd), dt) for wd, dt in zip(widths, out_dtypes)],
        scratch_shapes=[pltpu.VMEM((c, d, ch), BF16) for c in group_chunks],
        compiler_params=_params(("arbitrary",)),
        name="norm_project",
    )(x, g.reshape(1, d), w)


def _attn_body(q_ref, kc_ref, kp_ref, vc_ref, vp_ref, sink_ref, o_ref, *, blocks_per_seq):
    blk = ATTN_BLOCK
    n_sub = q_ref.shape[0] // blk
    step = pl.program_id(0)
    lane = lax.broadcasted_iota(jnp.int32, (2 * blk, LANES), 1)
    low = lane < HEAD_DIM
    qi = lax.broadcasted_iota(jnp.int32, (blk, 2 * blk), 0)
    kj = lax.broadcasted_iota(jnp.int32, (blk, 2 * blk), 1)
    rel = blk + qi - kj
    band = (rel >= 0) & (rel < blk)

    def placed(x, x_sw, kv_head, low_half):
        src = x if (kv_head == 0) == low_half else x_sw
        return jnp.where(low if low_half else jnp.logical_not(low), src, 0.0).astype(BF16)

    probs = [(j, kv, half) for j in range(n_sub) for kv in range(2) for half in range(2)]
    masks, k_ext, v_ext, qs = [], {}, {}, {}
    for j in range(n_sub):
        rows = slice(j * blk, (j + 1) * blk)
        prev = slice((j - 1) * blk, j * blk)
        kb = jnp.concatenate([kp_ref[...] if j == 0 else kc_ref[prev, :], kc_ref[rows, :]], axis=0).astype(F32)
        vb = jnp.concatenate([vp_ref[...] if j == 0 else vc_ref[prev, :], vc_ref[rows, :]], axis=0).astype(F32)
        kb_sw = pltpu.roll(kb, HEAD_DIM, 1)
        vb_sw = pltpu.roll(vb, HEAD_DIM, 1)
        first = ((step * n_sub + j) % blocks_per_seq == 0).astype(jnp.int32)
        masks.append(band & (kj >= first * blk))
        for kv in range(2):
            qs[j, kv] = ATTN_SCALE * jnp.concatenate([q_ref[rows, (2 * kv) * LANES:(2 * kv + 1) * LANES],
                                                      q_ref[rows, (2 * kv + 1) * LANES:(2 * kv + 2) * LANES]], axis=0)
            for half in range(2):
                k_ext[j, kv, half] = placed(kb, kb_sw, kv, half == 0)
                v_ext[j, kv, half] = placed(vb, vb_sw, kv, half == 0)
    scores = [_mm_nt(qs[j, kv], k_ext[j, kv, half]) for j, kv, half in probs]
    sinks = [jnp.max(sink_ref[h:h + 1, :], axis=-1, keepdims=True) for h in range(ATTN_HEADS)]
    weights = []
    for i, (j, kv, half) in enumerate(probs):
        parts = []
        for pair in range(2):
            sink = sinks[4 * kv + 2 * pair + half]
            s = jnp.where(masks[j], scores[i][pair * blk:(pair + 1) * blk], MASK_VALUE)
            m = jnp.maximum(jnp.max(s, axis=-1, keepdims=True), sink)
            e = jnp.exp(s - m)
            denom = jnp.sum(e, axis=-1, keepdims=True) + jnp.exp(sink - m)
            parts.append((e * (1.0 / denom)).astype(BF16))
        weights.append(jnp.concatenate(parts, axis=0))
    outs = [_mm(weights[i], v_ext[p]) for i, p in enumerate(probs)]
    for j in range(n_sub):
        for kv in range(2):
            i = (j * 2 + kv) * 2
            out = outs[i] + outs[i + 1]
            rows = slice(j * blk, (j + 1) * blk)
            o_ref[rows, (2 * kv) * LANES:(2 * kv + 1) * LANES] = out[:blk].astype(o_ref.dtype)
            o_ref[rows, (2 * kv + 1) * LANES:(2 * kv + 2) * LANES] = out[blk:].astype(o_ref.dtype)


def sliding_window_attention(z_attn, sinks, seq_len):
    n = z_attn.shape[0]
    blk = ATTN_BLOCK
    span = min(ATTN_STEP_BLOCKS * blk, seq_len)
    sub = span // blk
    qw = ATTN_HEADS * HEAD_DIM
    kcol = qw // LANES
    vcol = kcol + 1
    sink_rows = jnp.broadcast_to(sinks.astype(F32)[:, None], (ATTN_HEADS, LANES))
    prev = lambda i: jnp.maximum(i * sub - 1, 0)
    return pl.pallas_call(
        functools.partial(_attn_body, blocks_per_seq=seq_len // blk),
        grid=(n // span,),
        in_specs=[
            pl.BlockSpec((span, qw), lambda i: (i, 0)),
            pl.BlockSpec((span, LANES), lambda i: (i, kcol)),
            pl.BlockSpec((blk, LANES), lambda i: (prev(i), kcol)),
            pl.BlockSpec((span, LANES), lambda i: (i, vcol)),
            pl.BlockSpec((blk, LANES), lambda i: (prev(i), vcol)),
            _full((ATTN_HEADS, LANES)),
        ],
        out_specs=pl.BlockSpec((span, qw), lambda i: (i, 0)),
        out_shape=jax.ShapeDtypeStruct((n, qw), BF16),
        compiler_params=_params(("parallel",)),
        name="sliding_window_attention",
    )(z_attn, z_attn, z_attn, z_attn, z_attn, sink_rows)


def _head_ones(width, head):
    idx = jnp.arange(width) // head
    return (idx[:, None] == idx[None, :]).astype(BF16)


def _chunk_selectors(rows, chunk):
    r = jnp.arange(rows)
    prefix = (r[:, None] // chunk == r[None, :] // chunk) & (r[None, :] <= r[:, None])
    member = jnp.arange(rows // chunk)[:, None] == r[None, :] // chunk
    return prefix.astype(BF16), member.astype(BF16)


def _rwkv_pre_body(z_ref, zp_ref, mu_ref, w0_ref, wup_ref, a0_ref, aup_ref, gup_ref, kk_ref, ka_ref, rk_ref,
                   ones_ref, prefix_ref, member_ref,
                   ad_o, rd_o, bd_o, kd_o, be_o, ke_o, v_o, gam_o, bonus_o, g_o, *, tiles_per_seq):
    bw = BRANCH_WIDTH
    i = pl.program_id(0)
    z = z_ref[...]
    keep = (i % tiles_per_seq != 0).astype(F32)
    prev_row = zp_ref[7:8, :] * keep
    row = lax.broadcasted_iota(jnp.int32, z.shape, 0)
    z_prev = jnp.where(row == 0, prev_row, pltpu.roll(z, 1, 0))
    z = z + mu_ref[...] * (z_prev - z)
    r = z[:, 0:bw]
    k = z[:, bw:2 * bw]
    v = z[:, 2 * bw:3 * bw]
    x_wa = z[:, 3 * bw:3 * bw + LANES]
    x_g = z[:, 3 * bw + LANES:3 * bw + 2 * LANES]
    d = w0_ref[...] + _mm(jnp.tanh(x_wa), wup_ref[...])
    lw = -RWKV_DECAY_SCALE * jax.nn.sigmoid(d)
    a_gate = jax.nn.sigmoid(a0_ref[...] + _mm(x_wa, aup_ref[...]))
    g = _mm(jax.nn.sigmoid(x_g), gup_ref[...])
    kk = k * kk_ref[...]
    ones = ones_ref[...]
    kk = kk * lax.rsqrt(jnp.maximum(_mm(kk * kk, ones), 1e-24))
    k = k * (1.0 + (a_gate - 1.0) * ka_ref[...])
    chunk = RWKV_CHUNK
    tm = z.shape[0]
    cs = _sel_mm(prefix_ref[...], lw, pieces=2)
    tot_c = _sel_mm(member_ref[...], lw)
    gam_o[:, 0, :] = jnp.exp(tot_c)
    tot = jnp.broadcast_to(tot_c[:, None, :], (tm // chunk, chunk, bw)).reshape(tm, bw)
    dec_out = jnp.exp(-cs)
    dec_end = jnp.exp(tot - cs)
    b = kk * a_gate
    ad_o[...] = (-kk * jnp.exp(cs - lw)).astype(ad_o.dtype)
    rd_o[...] = (r * jnp.exp(cs)).astype(rd_o.dtype)
    bd_o[...] = (b * dec_out).astype(bd_o.dtype)
    kd_o[...] = (k * dec_out).astype(kd_o.dtype)
    be_o[...] = (b * dec_end).astype(be_o.dtype)
    ke_o[...] = (k * dec_end).astype(ke_o.dtype)
    v_o[...] = v.astype(v_o.dtype)
    bonus_o[...] = _mm(r * k * rk_ref[...], ones) * v
    g_o[...] = g


def rwkv_prepare(z, mu, w0, w_up, a0, a_up, g_up, k_k, k_a, r_k, seq_len):
    n, zw = z.shape
    bw = BRANCH_WIDTH
    tm = min(ROW_TILE, seq_len)
    half = w_up.shape[0]
    zeros = jnp.zeros((LANES - half, bw), BF16)
    wup_ext = jnp.concatenate([w_up.astype(BF16), zeros], axis=0)
    aup_ext = jnp.concatenate([zeros, a_up.astype(BF16)], axis=0)
    row = lambda t: t.reshape(1, -1).astype(F32)
    tile = pl.BlockSpec((tm, bw), lambda i: (i, 0))
    cpt = tm // RWKV_CHUNK
    return pl.pallas_call(
        functools.partial(_rwkv_pre_body, tiles_per_seq=seq_len // tm),
        grid=(n // tm,),
        in_specs=[
            pl.BlockSpec((tm, zw), lambda i: (i, 0)),
            pl.BlockSpec((8, zw), lambda i: (jnp.maximum(i * (tm // 8) - 1, 0), 0)),
            _full((1, zw)), _full((1, bw)), _full((LANES, bw)), _full((1, bw)), _full((LANES, bw)),
            _full((LANES, bw)), _full((1, bw)), _full((1, bw)), _full((1, bw)),
            _resident((bw, bw)), _resident((tm, tm)), _resident((cpt, tm)),
        ],
        out_specs=[tile] * 7 + [pl.BlockSpec((cpt, 1, bw), lambda i: (i, 0, 0)), tile, tile],
        out_shape=[jax.ShapeDtypeStruct((n, bw), BF16)] * 7
        + [jax.ShapeDtypeStruct((n // RWKV_CHUNK, 1, bw), F32)] + [jax.ShapeDtypeStruct((n, bw), F32)] * 2,
        compiler_params=_params(("parallel",)),
        name="rwkv_prepare",
    )(z, z, row(mu), row(w0), wup_ext, row(a0), aup_ext, g_up.astype(BF16), row(k_k), row(k_a), row(r_k),
      _head_ones(bw, HEAD_DIM), *_chunk_selectors(tm, RWKV_CHUNK))


def _rwkv_scan_body(ad_ref, rd_ref, bd_ref, kd_ref, be_ref, ke_ref, v_ref, gam_ref, y_ref, s_ref):
    c = pl.program_id(0)

    @pl.when(c == 0)
    def _():
        s_ref[...] = jnp.zeros_like(s_ref)

    batch = v_ref.shape[0]
    chunk = RWKV_CHUNK
    n_chunks = v_ref.shape[1] // chunk
    gh, gl = RWKV_GROUP_HEADS, RWKV_GROUP_LANES
    rows = gh * chunk
    row_head = lax.broadcasted_iota(jnp.int32, (rows, gl), 0) // chunk
    lane_head = lax.broadcasted_iota(jnp.int32, (rows, gl), 1) // HEAD_DIM
    own = row_head == lane_head
    ii = lax.broadcasted_iota(jnp.int32, (rows, rows), 0)
    jj = lax.broadcasted_iota(jnp.int32, (rows, rows), 1)
    strict = jj < ii
    incl = jj <= ii
    eye = (ii == jj).astype(F32)

    def stack(x):
        return jnp.where(own, jnp.concatenate([x] * gh, axis=0), 0.0).astype(BF16)

    chains = [(bi, g) for bi in range(batch) for g in range(BRANCH_WIDTH // gl)]
    probs = [(c, bi, g) for c in range(n_chunks) for bi, g in chains]
    npb = range(len(probs))
    operand_refs = (ad_ref, rd_ref, bd_ref, kd_ref, be_ref, ke_ref, v_ref)
    ar, bk, v_s, be_ke, tot = [], [], [], [], []
    for c, bi, g in probs:
        rs = slice(c * chunk, (c + 1) * chunk)
        sl = slice(g * gl, (g + 1) * gl)
        ad, rd, bd, kd, be, ke, v = (ref[bi, rs, sl] for ref in operand_refs)
        ar.append(jnp.concatenate([stack(ad), stack(rd)], axis=0))
        bk.append(jnp.concatenate([stack(bd), stack(kd)], axis=0))
        v_s.append(stack(v))
        be_ke.append(jnp.concatenate([stack(be), stack(ke)], axis=0))
        tot.append(gam_ref[bi, c, :, sl])
    gram = [_mm_nt(ar[i], bk[i]) for i in npb]
    nmat = [jnp.where(strict, gram[i][:rows, :rows], 0.0).astype(BF16) for i in npb]
    m_ak = [jnp.where(strict, gram[i][:rows, rows:], 0.0).astype(BF16) for i in npb]
    m_rbk = [jnp.where(jnp.concatenate([incl, incl], axis=1), gram[i][rows:, :], 0.0).astype(BF16) for i in npb]
    tinv = [eye + nmat[i].astype(F32) for i in npb]
    for _ in range(chunk.bit_length() - 2):
        nmat = [_mm(nmat[i], nmat[i]).astype(BF16) for i in npb]
        tinv = [tinv[i] + _mm(tinv[i], nmat[i]) for i in npb]
    tinv = [tinv[i].astype(BF16) for i in npb]
    x0 = [_mm(m_ak[i], v_s[i]) for i in npb]

    sts = [s_ref[bi, g] for bi, g in chains]
    ncn = len(chains)
    for c in range(n_chunks):
        ids = [c * ncn + j for j in range(ncn)]
        ars = [_mm_nt(ar[i], sts[j]) for j, i in enumerate(ids)]
        sa = [_mm(tinv[i], ars[j][:rows] + x0[i]).astype(BF16) for j, i in enumerate(ids)]
        sv = [jnp.concatenate([sa[j], v_s[i]], axis=0) for j, i in enumerate(ids)]
        y_s = [ars[j][rows:] + _mm(m_rbk[i], sv[j]) for j, i in enumerate(ids)]
        upd = [_mm_tn(sv[j], be_ke[i]) for j, i in enumerate(ids)]
        sts = [sts[j] * tot[i] + upd[j] for j, i in enumerate(ids)]
        for j, (bi, g) in enumerate(chains):
            y = y_s[j][0:chunk]
            for h in range(1, gh):
                y = y + y_s[j][h * chunk:(h + 1) * chunk]
            y_ref[bi, c * chunk:(c + 1) * chunk, g * gl:(g + 1) * gl] = y
    for j, (bi, g) in enumerate(chains):
        s_ref[bi, g] = sts[j]


def rwkv_scan(ad, rd, bd, kd, be, ke, v, gam):
    bsz, seq, bw = v.shape
    steps = min(RWKV_STEP_CHUNKS, seq // RWKV_CHUNK)
    span = steps * RWKV_CHUNK
    blk = pl.BlockSpec((bsz, span, bw), lambda c: (0, c, 0))
    groups = bw // RWKV_GROUP_LANES
    return pl.pallas_call(
        _rwkv_scan_body,
        grid=(seq // span,),
        in_specs=[blk] * 7 + [pl.BlockSpec((bsz, steps, 1, bw), lambda c: (0, c, 0, 0))],
        out_specs=blk,
        out_shape=jax.ShapeDtypeStruct((bsz, seq, bw), F32),
        scratch_shapes=[pltpu.VMEM((bsz, groups, RWKV_GROUP_LANES, RWKV_GROUP_LANES), F32)],
        compiler_params=_params(("arbitrary",)),
        name="rwkv_scan",
    )(ad, rd, bd, kd, be, ke, v, gam)


def _hgrn_body(z_ref, lb_ref, ng_ref, o_ref, s_ref, c_ref):
    c = pl.program_id(0)

    @pl.when(c == 0)
    def _():
        s_ref[...] = jnp.zeros_like(s_ref)

    batch, rows, _ = z_ref.shape
    bw = BRANCH_WIDTH
    hd = HGRN_DIM
    ch = HGRN_CHUNK
    half = ch // 2
    upto = (lax.broadcasted_iota(jnp.int32, (ch, ch), 1) <= lax.broadcasted_iota(jnp.int32, (ch, ch), 0))
    row8 = lax.broadcasted_iota(jnp.int32, (half, hd), 0)
    lane8 = lax.broadcasted_iota(jnp.int32, (half, hd), 1)
    chains = [(bi, h) for bi in range(batch) for h in range(HGRN_HEADS)]

    def gates(rs, slot):
        lb = lb_ref[...]
        q, qe, ke, gam, og, b2, val = [], [], [], [], [], [], []
        for bi in range(batch):
            zq = z_ref[bi, rs, 0:bw]
            zf = z_ref[bi, rs, bw:2 * bw]
            zo = z_ref[bi, rs, 3 * bw:4 * bw]
            forget = lb + (1.0 - lb) * jax.nn.sigmoid(zf)
            key = 1.0 - forget
            b = _sel_mm(upto, jnp.log(forget))
            tot = b[ch - 1:ch, :]
            qb = zq * jax.nn.sigmoid(zq)
            q.append(qb)
            qe.append((qb * jnp.exp(b)).astype(BF16))
            ke.append((key * jnp.exp(tot - b)).astype(BF16))
            gam.append(jnp.exp(tot))
            og.append(zo * jax.nn.sigmoid(zo))
            val.append(z_ref[bi, rs, 2 * bw:3 * bw].astype(BF16))
            b2.append(b * LOG2_E)
            c_ref[slot, bi] = (b - jnp.log(key)) * LOG2_E
        return q, qe, ke, gam, og, b2, val

    def chunk_weights(q, b2, slot):
        attn = []
        for bi, h in chains:
            hs = slice(h * hd, (h + 1) * hd)
            b_lo = b2[bi][0:half, hs]
            b_hi = b2[bi][half:ch, hs]
            q_lo = q[bi][0:half, hs]
            q_hi = q[bi][half:ch, hs]
            a_lo = jnp.zeros((half, hd), F32)
            a_hi = jnp.zeros((half, hd), F32)
            for s in range(ch):
                c_s = c_ref[slot, bi, s:s + 1, hs]
                if s < half:
                    w = q_lo * jnp.exp2(jnp.where(row8 >= s, b_lo - c_s, MASK_VALUE))
                    a_lo = jnp.where(lane8 == s, jnp.sum(w, axis=-1, keepdims=True), a_lo)
                    w = q_hi * jnp.exp2(b_hi - c_s)
                else:
                    w = q_hi * jnp.exp2(jnp.where(row8 >= s - half, b_hi - c_s, MASK_VALUE))
                a_hi = jnp.where(lane8 == s, jnp.sum(w, axis=-1, keepdims=True), a_hi)
            attn.append(jnp.concatenate([a_lo, a_hi], axis=0)[:, 0:ch].astype(BF16))
        return attn

    def chunk_group(jj, carry):
        slots = range(HGRN_UNROLL)
        rss = [pl.ds(pl.multiple_of((jj * HGRN_UNROLL + u) * ch, ch), ch) for u in slots]
        gated = [gates(rss[u], u) for u in slots]
        attn = [chunk_weights(gated[u][0], gated[u][5], u) for u in slots]
        sts = [s_ref[bi, h] for bi, h in chains]
        outs = []
        for u in slots:
            _, qe, ke, gam, og, _, val = gated[u]
            hsl = [slice(h * hd, (h + 1) * hd) for _, h in chains]
            inter = [_mm_nt(qe[bi][:, hsl[i]], sts[i]) for i, (bi, _) in enumerate(chains)]
            upd = [_mm_tn(val[bi][:, hsl[i]], ke[bi][:, hsl[i]]) for i, (bi, _) in enumerate(chains)]
            intra = [_mm(attn[u][i], val[bi][:, hsl[i]]) for i, (bi, _) in enumerate(chains)]
            outs.append([_rms(inter[i] + intra[i], ng_ref[...]) * og[bi][:, hsl[i]]
                         for i, (bi, _) in enumerate(chains)])
            sts = [sts[i] * gam[bi][:, hsl[i]] + upd[i] for i, (bi, _) in enumerate(chains)]
        for u in slots:
            for i, (bi, h) in enumerate(chains):
                o_ref[bi, rss[u], h * hd:(h + 1) * hd] = outs[u][i].astype(o_ref.dtype)
        for i, (bi, h) in enumerate(chains):
            s_ref[bi, h] = sts[i]
        return carry

    lax.fori_loop(0, rows // (ch * HGRN_UNROLL), chunk_group, 0)


def hgrn_mix(z, lb, norm_g):
    bsz, seq, zw = z.shape
    bw = BRANCH_WIDTH
    rows = min(HGRN_BLOCK, seq)
    blk = pltpu.VMEM((HGRN_UNROLL, bsz, HGRN_CHUNK, bw), F32)
    return pl.pallas_call(
        _hgrn_body,
        grid=(seq // rows,),
        in_specs=[pl.BlockSpec((bsz, rows, zw), lambda c: (0, c, 0)), _full((1, bw)), _full((1, HGRN_DIM))],
        out_specs=pl.BlockSpec((bsz, rows, bw), lambda c: (0, c, 0)),
        out_shape=jax.ShapeDtypeStruct((bsz, seq, bw), BF16),
        scratch_shapes=[pltpu.VMEM((bsz, HGRN_HEADS, HGRN_DIM, HGRN_DIM), F32), blk],
        compiler_params=_params(("arbitrary",)),
        name="hgrn_mix",
    )(z, lb.reshape(1, bw).astype(F32), norm_g.reshape(1, HGRN_DIM).astype(F32))


def _merge_body(x_ref, g_ref, ya_ref, y_ref, bonus_ref, gg_ref, lnw_ref, lnb_ref, ones_ref, yc_ref, wgate_ref, wb_ref,
                wo_ref, o_ref, wgate_s, wb_s, wo_s, *, chunks):
    i = pl.program_id(0)
    n_branch, wo_chunks = wb_s.shape[0], wo_s.shape[0]

    @pl.when(i < chunks)
    def _():
        wgate_s[i] = wgate_ref[...].astype(BF16)

        @pl.when(i < n_branch)
        def _():
            wb_s[i] = wb_ref[0].astype(BF16)

        @pl.when(i < wo_chunks)
        def _():
            wo_s[i] = wo_ref[...].astype(BF16)

    @pl.when(i >= chunks)
    def _():
        x = x_ref[...]
        d = x.shape[1]
        per = d // WEIGHT_CHUNK
        h = _rms(x, g_ref[...]).astype(BF16)
        ones = ones_ref[...]
        y = y_ref[...]
        yc = y - _mm(y, ones) * (1.0 / HEAD_DIM)
        var = _mm(yc * yc, ones) * (1.0 / HEAD_DIM)
        yn = yc * lax.rsqrt(var + RWKV_GN_EPS) * lnw_ref[...] + lnb_ref[...]
        yb = ((yn + bonus_ref[...]) * gg_ref[...]).astype(BF16)
        merged = jnp.zeros(x.shape, F32)
        for n, branch in enumerate((ya_ref[...], yb, yc_ref[...])):
            gate = jnp.concatenate([jnp.dot(h, wgate_s[n * per + c], preferred_element_type=F32)
                                    for c in range(per)], axis=1)
            merged = merged + jax.nn.sigmoid(gate) * jnp.dot(branch, wb_s[n], preferred_element_type=F32)
        merged = merged.astype(BF16)
        out = x
        for c in range(wo_chunks):
            rows = slice(c * WEIGHT_CHUNK, (c + 1) * WEIGHT_CHUNK)
            out = out + jnp.dot(merged[:, rows], wo_s[c], preferred_element_type=F32)
        o_ref[...] = out


def merge_branches(x, g, ya, y, bonus, gg, ln_w, ln_b, yc, w_in, gate_col, w_branch, w_o, layer):
    n, d = x.shape
    bw = ya.shape[1]
    nb = w_branch.shape[1]
    tm = min(ROW_TILE, n)
    ch = WEIGHT_CHUNK
    chunks = nb * d // ch
    rows = lambda i: (jnp.maximum(i - chunks, 0), 0)
    ytile = pl.BlockSpec((tm, bw), rows)
    return pl.pallas_call(
        functools.partial(_merge_body, chunks=chunks),
        grid=(chunks + n // tm,),
        in_specs=[pl.BlockSpec((tm, d), rows), _full((1, d)), ytile, ytile, ytile, ytile, _full((1, bw)), _full((1, bw)),
                  _resident((bw, bw)), ytile,
                  pl.BlockSpec((None, d, ch), lambda i: (layer, 0, gate_col // ch + jnp.minimum(i, chunks - 1))),
                  pl.BlockSpec((None, 1, bw, d), lambda i: (layer, jnp.minimum(i, nb - 1), 0, 0)),
                  pl.BlockSpec((None, ch, d), lambda i: (layer, jnp.minimum(i, d // ch - 1), 0))],
        out_specs=pl.BlockSpec((tm, d), rows),
        out_shape=jax.ShapeDtypeStruct((n, d), F32),
        scratch_shapes=[pltpu.VMEM((chunks, d, ch), BF16), pltpu.VMEM((nb, bw, d), BF16),
                        pltpu.VMEM((d // ch, ch, d), BF16)],
        compiler_params=_params(("arbitrary",)),
        name="merge_branches",
    )(x, g.reshape(1, d), ya, y, bonus, gg, ln_w.reshape(1, bw).astype(F32), ln_b.reshape(1, bw).astype(F32),
      _head_ones(bw, HEAD_DIM), yc, w_in, w_branch, w_o)


def token_mixer(x, seq_len, layer, mix_norm, w_in, attn_sinks, rwkv_mu, rwkv_w0, rwkv_w_up, rwkv_a0, rwkv_a_up, rwkv_g_up,
                rwkv_k_k, rwkv_k_a, rwkv_r_k, rwkv_ln_w, rwkv_ln_b, lb, hgrn_norm, w_branch, w_o):
    n, d = x.shape
    bsz = n // seq_len
    bw = BRANCH_WIDTH
    attn_w = (ATTN_HEADS + 2 * (ATTN_HEADS // 4)) * HEAD_DIM
    rwkv_w = rwkv_mu.shape[0]
    hgrn_w = 4 * bw
    o3 = attn_w + rwkv_w + hgrn_w
    z_attn, z_rwkv, z_hgrn = norm_project(x, mix_norm, w_in, layer, 0, (attn_w, rwkv_w, hgrn_w), (BF16, F32, F32))

    y_a = sliding_window_attention(z_attn, attn_sinks, seq_len)

    *operands, gam, bonus, g = rwkv_prepare(z_rwkv, rwkv_mu, rwkv_w0, rwkv_w_up, rwkv_a0, rwkv_a_up, rwkv_g_up,
                                            rwkv_k_k, rwkv_k_a, rwkv_r_k, seq_len)
    y = rwkv_scan(*(t.reshape(bsz, seq_len, bw) for t in operands),
                  gam.reshape(bsz, seq_len // RWKV_CHUNK, 1, bw)).reshape(n, bw)
    y_c = hgrn_mix(z_hgrn.reshape(bsz, seq_len, hgrn_w), lb, hgrn_norm).reshape(n, bw)

    return merge_branches(x, mix_norm, y_a, y, bonus, g, rwkv_ln_w, rwkv_ln_b, y_c, w_in, o3, w_branch, w_o, layer)


def kernel(x, p, ffn1_norm, ffn1_w_gate, ffn1_w_up, ffn1_w_down, mix_norm, w_in, attn_sinks, rwkv_mu, rwkv_w0,
           rwkv_w_up, rwkv_a0, rwkv_a_up, rwkv_g_up, rwkv_k_k, rwkv_k_a, rwkv_r_k, rwkv_ln_w, rwkv_ln_b, hgrn_lb,
           hgrn_norm, w_branch, w_o, ffn2_norm, ffn2_w_gate, ffn2_w_up, ffn2_w_down, ple_norm, ple_w_gate,
           ple_w_proj, final_norm):
    bsz, seq_len, d = x.shape
    depth = p.shape[0]
    n = bsz * seq_len
    lb_all = jax.nn.softmax(hgrn_lb.astype(F32), axis=0)
    lb_layers = jnp.clip(jnp.cumsum(lb_all, axis=0) - lb_all[0:1], 0.0, 1.0 - 1e-6)
    x = x.reshape(n, d)
    for i in range(depth):
        x = ffn_half_step(x, ffn1_norm[i], ffn1_w_gate, ffn1_w_up, ffn1_w_down, i)
        x = token_mixer(x, seq_len, i, mix_norm[i], w_in, attn_sinks[i], rwkv_mu[i], rwkv_w0[i], rwkv_w_up[i],
                        rwkv_a0[i], rwkv_a_up[i], rwkv_g_up[i], rwkv_k_k[i], rwkv_k_a[i], rwkv_r_k[i],
                        rwkv_ln_w[i], rwkv_ln_b[i], lb_layers[i], hgrn_norm[i], w_branch, w_o)
        ple = (p.reshape(depth, n, -1), ple_norm[i], ple_w_gate, ple_w_proj, final_norm)
        x = ffn_half_step(x, ffn2_norm[i], ffn2_w_gate, ffn2_w_up, ffn2_w_down, i, ple=ple, final=(i == depth - 1))
    return x.reshape(bsz, seq_len, d)
```
